```python
import math
import jax, jax.numpy as jnp
from jax import lax
import numpy as np

D_MODEL = 1024
BATCH = 32
SEQ = 2048
DEPTH = 1

ATTN_HEADS = 8
HEAD_DIM = 64
ATTN_WIDTH = ATTN_HEADS * HEAD_DIM
LRU_WIDTH = D_MODEL - ATTN_WIDTH
LRU_BLOCKS = 8
LRU_BLOCK_DIM = LRU_WIDTH // LRU_BLOCKS
MIX_WIDTH = ATTN_WIDTH + LRU_WIDTH
IN_WIDTH = 4 * ATTN_WIDTH + 2 * LRU_WIDTH
MOBA_BLOCK = 256
MOBA_TOPK = 3
QUERY_CHUNK = 8
CONV_WIDTH = 4
LRU_C = 8.0
ROPE_THETA = 10000.0
EPS = 1e-6

kernel_name = "hymba_moba_rglru_block"


def rmsnorm(x, gain):
    xf = x.astype(jnp.float32)
    y = xf * lax.rsqrt(jnp.mean(xf * xf, axis=-1, keepdims=True) + EPS)
    return (y * gain.astype(jnp.float32)).astype(x.dtype)


def rope_tables(seq_len):
    pos = jnp.arange(seq_len, dtype=jnp.float32)
    inv_freq = ROPE_THETA ** (-jnp.arange(0, HEAD_DIM, 2, dtype=jnp.float32) / HEAD_DIM)
    ang = pos[:, None] * inv_freq[None, :]
    return jnp.cos(ang), jnp.sin(ang)


def apply_rope(t, cos, sin):
    tf = t.astype(jnp.float32)
    t1, t2 = tf[..., : HEAD_DIM // 2], tf[..., HEAD_DIM // 2:]
    c, s = cos[None, :, None, :], sin[None, :, None, :]
    return jnp.concatenate([t1 * c - t2 * s, t2 * c + t1 * s], axis=-1).astype(t.dtype)


def moba_attention(q, k, v):
    B, S, H, Dh = q.shape
    n_blk = -(-S // MOBA_BLOCK)
    s_pad = n_blk * MOBA_BLOCK
    pad = ((0, 0), (0, s_pad - S), (0, 0), (0, 0))
    q, k, v = [jnp.pad(t, pad).transpose(0, 2, 1, 3) for t in (q, k, v)]
    k_blk = k.reshape(B, H, n_blk, MOBA_BLOCK, Dh)
    v_blk = v.reshape(B, H, n_blk, MOBA_BLOCK, Dh)
    k_mean = jnp.mean(k_blk.astype(jnp.float32), axis=3).astype(q.dtype)

    q_blk_id = jnp.arange(s_pad) // MOBA_BLOCK
    gate = jnp.einsum('bhsd,bhnd->bhsn', q, k_mean).astype(jnp.float32)
    fully_past = jnp.arange(n_blk)[None, :] < q_blk_id[:, None]
    gate = jnp.where(fully_past, gate, -jnp.inf)
    k_eff = min(MOBA_TOPK, n_blk)
    _, sel_idx = lax.top_k(gate, k_eff)
    n_valid = jnp.minimum(q_blk_id, k_eff)
    sel_valid = jnp.arange(k_eff)[None, :] < n_valid[:, None]

    scale = Dh ** -0.5
    gather_blocks = jax.vmap(jax.vmap(lambda blocks, idx: blocks[idx]))

    def attend_chunk(ci):
        start = ci * QUERY_CHUNK
        q_c = lax.dynamic_slice_in_dim(q, start, QUERY_CHUNK, axis=2)
        idx_c = lax.dynamic_slice_in_dim(sel_idx, start, QUERY_CHUNK, axis=2)
        valid_c = lax.dynamic_slice_in_dim(sel_valid, start, QUERY_CHUNK, axis=0)
        blk_start = (start // MOBA_BLOCK) * MOBA_BLOCK
        k_own = lax.dynamic_slice_in_dim(k, blk_start, MOBA_BLOCK, axis=2)
        v_own = lax.dynamic_slice_in_dim(v, blk_start, MOBA_BLOCK, axis=2)
        qpos = start + jnp.arange(QUERY_CHUNK)
        kpos = blk_start + jnp.arange(MOBA_BLOCK)
        s_own = jnp.einsum('bhqd,bhkd->bhqk', q_c, k_own).astype(jnp.float32) * scale
        s_own = jnp.where(kpos[None, :] <= qpos[:, None], s_own, -jnp.inf)
        k_g = gather_blocks(k_blk, idx_c)
        v_g = gather_blocks(v_blk, idx_c)
        s_sel = jnp.einsum('bhqd,bhqjkd->bhqjk', q_c, k_g).astype(jnp.float32) * scale
        s_sel = jnp.where(valid_c[:, :, None], s_sel, -jnp.inf)
        s_sel = s_sel.reshape(B, H, QUERY_CHUNK, k_eff * MOBA_BLOCK)
        p = jax.nn.softmax(jnp.concatenate([s_own, s_sel], axis=-1), axis=-1).astype(v.dtype)
        p_own = p[..., :MOBA_BLOCK]
        p_sel = p[..., MOBA_BLOCK:].reshape(B, H, QUERY_CHUNK, k_eff, MOBA_BLOCK)
        return (jnp.einsum('bhqk,bhkd->bhqd', p_own, v_own)
                + jnp.einsum('bhqjk,bhqjkd->bhqd', p_sel, v_g))

    n_chunks = s_pad // QUERY_CHUNK
    out = lax.map(attend_chunk, jnp.arange(n_chunks))
    out = out.transpose(1, 0, 3, 2, 4).reshape(B, s_pad, H, Dh)
    return out[:, :S]


def causal_depthwise_conv(x, w, b):
    out = lax.conv_general_dilated(
        x, w[:, None, :], window_strides=(1,), padding=[(CONV_WIDTH - 1, 0)],
        dimension_numbers=('NWC', 'WIO', 'NWC'), feature_group_count=x.shape[-1])
    return out + b


def rg_lru(x, w_r, b_r, w_i, b_i, lam):
    B, S, W = x.shape
    xb = x.reshape(B, S, LRU_BLOCKS, LRU_BLOCK_DIM)
    r = jax.nn.sigmoid((jnp.einsum('bsgi,gij->bsgj', xb, w_r) + b_r).astype(jnp.float32)).reshape(B, S, W)
    i = jax.nn.sigmoid((jnp.einsum('bsgi,gij->bsgj', xb, w_i) + b_i).astype(jnp.float32)).reshape(B, S, W)
    log_a = -LRU_C * r * jax.nn.softplus(-lam.astype(jnp.float32))
    a = jnp.exp(log_a)
    norm = jnp.sqrt(-jnp.expm1(2.0 * log_a))
    u = norm * i * x.astype(jnp.float32)

    def combine(left, right):
        a1, b1 = left
        a2, b2 = right
        return a1 * a2, a2 * b1 + b2

    _, h = lax.associative_scan(combine, (a, u), axis=1)
    return h.astype(x.dtype)


def setup_inputs(seed: int = 0) -> dict:
    key = jax.random.key(seed)
    ks = jax.random.split(key, 20)
    f32 = jnp.float32
    nrm = lambda k, shape, s: jax.random.normal(k, shape, f32) * s
    u = jax.random.uniform(ks[13], (DEPTH, LRU_WIDTH), f32, 0.9, 0.999)
    a0 = u ** (1.0 / LRU_C)
    lru_lambda = jnp.log(a0) - jnp.log1p(-a0)
    return {
        "x": nrm(ks[0], (BATCH, SEQ, D_MODEL), 1.0),
        "c": nrm(ks[1], (BATCH, D_MODEL), 1.0),
        "w_mod": nrm(ks[2], (DEPTH, D_MODEL, 3 * D_MODEL), 0.5 * D_MODEL ** -0.5),
        "b_mod": nrm(ks[3], (DEPTH, 3 * D_MODEL), 0.01),
        "norm_gain": 1.0 + nrm(ks[4], (DEPTH, D_MODEL), 0.02),
        "w_in": nrm(ks[5], (DEPTH, D_MODEL, IN_WIDTH), D_MODEL ** -0.5),
        "conv_w": nrm(ks[6], (DEPTH, CONV_WIDTH, LRU_WIDTH), CONV_WIDTH ** -0.5),
        "conv_b": nrm(ks[7], (DEPTH, LRU_WIDTH), 0.01),
        "w_rgate": nrm(ks[8], (DEPTH, LRU_BLOCKS, LRU_BLOCK_DIM, LRU_BLOCK_DIM), LRU_BLOCK_DIM ** -0.5),
        "b_rgate": nrm(ks[9], (DEPTH, LRU_BLOCKS, LRU_BLOCK_DIM), 0.01),
        "w_igate": nrm(ks[10], (DEPTH, LRU_BLOCKS, LRU_BLOCK_DIM, LRU_BLOCK_DIM), LRU_BLOCK_DIM ** -0.5),
        "b_igate": nrm(ks[11], (DEPTH, LRU_BLOCKS, LRU_BLOCK_DIM), 0.01),
        "lru_lambda": lru_lambda,
        "attn_out_gain": 1.0 + nrm(ks[14], (DEPTH, ATTN_WIDTH), 0.02),
        "lru_out_gain": 1.0 + nrm(ks[15], (DEPTH, LRU_WIDTH), 0.02),
        "w_out": nrm(ks[16], (DEPTH, MIX_WIDTH, D_MODEL), MIX_WIDTH ** -0.5),
        "final_gain": 1.0 + nrm(ks[17], (D_MODEL,), 0.02),
    }


def reference(x, c, w_mod, b_mod, norm_gain, w_in, conv_w, conv_b, w_rgate, b_rgate,
              w_igate, b_igate, lru_lambda, attn_out_gain, lru_out_gain, w_out, final_gain):
    B, S, _ = x.shape
    cos, sin = rope_tables(S)
    c_act = jax.nn.silu(c)
    splits = [ATTN_WIDTH, 2 * ATTN_WIDTH, 3 * ATTN_WIDTH, 4 * ATTN_WIDTH, 4 * ATTN_WIDTH + LRU_WIDTH]
    for l in range(DEPTH):
        mod = c_act @ w_mod[l] + b_mod[l]
        shift, scale, gate = jnp.split(mod, 3, axis=-1)
        h = rmsnorm(x, norm_gain[l]) * (1.0 + scale[:, None, :]) + shift[:, None, :]
        proj = h @ w_in[l]
        q, k, v, z_attn, x_lru, z_lru = jnp.split(proj, splits, axis=-1)

        q = apply_rope(q.reshape(B, S, ATTN_HEADS, HEAD_DIM), cos, sin)
        k = apply_rope(k.reshape(B, S, ATTN_HEADS, HEAD_DIM), cos, sin)
        v = v.reshape(B, S, ATTN_HEADS, HEAD_DIM)
        attn = moba_attention(q, k, v).reshape(B, S, ATTN_WIDTH)
        y_attn = rmsnorm(attn, attn_out_gain[l]) * jax.nn.silu(z_attn)

        xc = causal_depthwise_conv(x_lru, conv_w[l], conv_b[l])
        rec = rg_lru(xc, w_rgate[l], b_rgate[l], w_igate[l], b_igate[l], lru_lambda[l])
        y_lru = rmsnorm(rec, lru_out_gain[l]) * jax.nn.silu(z_lru)

        y = jnp.concatenate([y_attn, y_lru], axis=-1) @ w_out[l]
        x = x + gate[:, None, :] * y
    return rmsnorm(x, final_gain)
```

```python
import functools

import jax
import jax.numpy as jnp
from jax import lax
from jax.experimental import pallas as pl
from jax.experimental.pallas import tpu as pltpu

ATTN_HEADS = 8
HEAD_DIM = 64
HALF_DIM = HEAD_DIM // 2
ATTN_WIDTH = ATTN_HEADS * HEAD_DIM
LRU_BLOCKS = 8
MOBA_BLOCK = 256
MOBA_TOPK = 3
CONV_WIDTH = 4
LRU_C = 8.0
ROPE_THETA = 10000.0
EPS = 1e-6

LANES = 128
SUBLANES = 8
BF16_ROWS = 16
MASK_VALUE = -1e30
VMEM_LIMIT = 56 * 1024 * 1024

F32 = jnp.float32
BF16 = jnp.bfloat16


def _sigmoid(z):
    return 1.0 / (1.0 + jnp.exp(-z))


def _split3(a):
    hi = a.astype(BF16)
    r1 = a - hi.astype(F32)
    mid = r1.astype(BF16)
    lo = (r1 - mid.astype(F32)).astype(BF16)
    return hi, mid, lo


def _mod_kernel(c_ref, w_ref, b_ref, o_ref):
    c = c_ref[...]
    act = c * _sigmoid(c)
    a_hi, a_mid, _ = _split3(act)
    w_hi, w_mid, _ = _split3(w_ref[...])
    dot = functools.partial(jnp.dot, preferred_element_type=F32)
    acc = dot(a_hi, w_hi) + dot(a_hi, w_mid) + dot(a_mid, w_hi)
    o_ref[...] = acc + b_ref[...]


def _modulation(c, w_mod, b_mod):
    bsz, d = c.shape
    n = w_mod.shape[1]
    tn = 512
    return pl.pallas_call(
        _mod_kernel,
        grid=(n // tn,),
        in_specs=[
            pl.BlockSpec((bsz, d), lambda j: (0, 0)),
            pl.BlockSpec((d, tn), lambda j: (0, j)),
            pl.BlockSpec((1, tn), lambda j: (0, j)),
        ],
        out_specs=pl.BlockSpec((bsz, tn), lambda j: (0, j)),
        out_shape=jax.ShapeDtypeStruct((bsz, n), F32),
        compiler_params=pltpu.CompilerParams(
            dimension_semantics=("arbitrary",), vmem_limit_bytes=VMEM_LIMIT),
        name="modulation",
    )(c, w_mod, b_mod.reshape(1, n))


def _inproj_kernel(x_ref, mod_ref, gain_ref, wn_ref, wt_ref, cosn_ref, sinn_ref,
                   cost_ref, sint_ref,
                   qt_ref, vt_ref, zat_ref, k_ref, xl_ref, zl_ref):
    d = x_ref.shape[2]
    w = ATTN_WIDTH
    x = x_ref[0]
    shift = mod_ref[0, :, 0:d]
    scale = mod_ref[0, :, d:2 * d]
    rs = lax.rsqrt(jnp.mean(x * x, axis=-1, keepdims=True) + EPS)
    h = (x * rs) * (gain_ref[...] * (1.0 + scale)) + shift
    hb = h.astype(BF16)

    k = jnp.dot(hb, wn_ref[:, 0:w], preferred_element_type=F32)
    cosn = cosn_ref[...]
    sinn = sinn_ref[...]
    lane = lax.broadcasted_iota(jnp.int32, cosn.shape, 1)
    first_half = (lane % HEAD_DIM) < HALF_DIM
    for g in range(w // LANES):
        kg = k[:, g * LANES:(g + 1) * LANES]
        swapped = jnp.where(first_half,
                            pltpu.roll(kg, LANES - HALF_DIM, axis=1),
                            pltpu.roll(kg, HALF_DIM, axis=1))
        k_ref[0, :, g * LANES:(g + 1) * LANES] = (kg * cosn + swapped * sinn).astype(BF16)
    xl_ref[0] = jnp.dot(hb, wn_ref[:, w:2 * w], preferred_element_type=F32).astype(BF16)
    zl_ref[0] = jnp.dot(hb, wn_ref[:, 2 * w:3 * w], preferred_element_type=F32).astype(BF16)

    nt = (((1,), (1,)), ((), ()))
    qt = lax.dot_general(wt_ref[0:w, :], hb, nt, preferred_element_type=F32)
    cost = cost_ref[...]
    sint = sint_ref[...]
    qscale = HEAD_DIM ** -0.5
    for hd in range(ATTN_HEADS):
        t = qt[hd * HEAD_DIM:(hd + 1) * HEAD_DIM]
        swapped = jnp.concatenate([t[HALF_DIM:], t[:HALF_DIM]], axis=0)
        qt_ref[0, hd * HEAD_DIM:(hd + 1) * HEAD_DIM, :] = (
            (t * cost + swapped * sint) * qscale).astype(BF16)
    vt_ref[0] = lax.dot_general(wt_ref[w:2 * w, :], hb, nt,
                                preferred_element_type=F32).astype(BF16)
    zat_ref[0] = lax.dot_general(wt_ref[2 * w:3 * w, :], hb, nt,
                                 preferred_element_type=F32).astype(BF16)


def _inproj(x, mod3, gain, w_nat, w_t, cosn, sinn, cost, sint, ts):
    bsz, s, d = x.shape
    w = ATTN_WIDTH
    fm = jax.ShapeDtypeStruct((bsz, w, s), BF16)
    sm = jax.ShapeDtypeStruct((bsz, s, w), BF16)
    fm_spec = pl.BlockSpec((1, w, ts), lambda b, i: (b, 0, i))
    sm_spec = pl.BlockSpec((1, ts, w), lambda b, i: (b, i, 0))
    return pl.pallas_call(
        _inproj_kernel,
        grid=(bsz, s // ts),
        in_specs=[
            pl.BlockSpec((1, ts, d), lambda b, i: (b, i, 0)),
            pl.BlockSpec((1, 1, 3 * d), lambda b, i: (b, 0, 0)),
            pl.BlockSpec((1, d), lambda b, i: (0, 0)),
            pl.BlockSpec((d, 3 * w), lambda b, i: (0, 0)),
            pl.BlockSpec((3 * w, d), lambda b, i: (0, 0)),
            pl.BlockSpec((ts, LANES), lambda b, i: (i, 0)),
            pl.BlockSpec((ts, LANES), lambda b, i: (i, 0)),
            pl.BlockSpec((HEAD_DIM, ts), lambda b, i: (0, i)),
            pl.BlockSpec((HEAD_DIM, ts), lambda b, i: (0, i)),
        ],
        out_specs=[fm_spec, fm_spec, fm_spec, sm_spec, sm_spec, sm_spec],
        out_shape=[fm, fm, fm, sm, sm, sm],
        compiler_params=pltpu.CompilerParams(
            dimension_semantics=("arbitrary", "arbitrary"), vmem_limit_bytes=VMEM_LIMIT),
        name="inproj",
    )(x, mod3, gain, w_nat, w_t, cosn, sinn, cost, sint)


def _attn_kernel(qt_ref, k_ref, vt_ref, o_ref, ka_ref, qa_ref):
    s = qt_ref.shape[2]
    nblk = s // MOBA_BLOCK
    par = pl.program_id(1) % 2
    head_lo = pl.multiple_of(par * HEAD_DIM, HEAD_DIM)
    aux_lo = pl.multiple_of(HEAD_DIM - par * HEAD_DIM, HEAD_DIM)

    kp = k_ref[0].astype(F32)
    lane = lax.broadcasted_iota(jnp.int32, (s, LANES), 1)
    row = lax.broadcasted_iota(jnp.int32, (s, LANES), 0)
    in_head = (lane >= head_lo) & (lane < head_lo + HEAD_DIM)
    onehot = ((lane - aux_lo) == (row // MOBA_BLOCK)).astype(F32)
    ka_ref[...] = jnp.where(in_head, kp, onehot).astype(BF16)

    qa_ref[...] = jnp.zeros(qa_ref.shape, BF16)
    qa_ref[pl.ds(head_lo, HEAD_DIM), :] = qt_ref[0]

    arow = lax.broadcasted_iota(jnp.int32, (BF16_ROWS, s), 0)
    acol = lax.broadcasted_iota(jnp.int32, (BF16_ROWS, s), 1)
    avg = jnp.where(arow == acol // MOBA_BLOCK, 1.0 / MOBA_BLOCK, 0.0).astype(BF16)
    kmean = jnp.dot(avg, k_ref[0], preferred_element_type=F32)
    km3 = jnp.concatenate(_split3(kmean), axis=0)
    g3 = jnp.dot(km3, qa_ref[...], preferred_element_type=F32)
    gate = (g3[0:BF16_ROWS] + g3[BF16_ROWS:2 * BF16_ROWS]) + g3[2 * BF16_ROWS:]
    gate = gate[0:nblk]

    jrow = lax.broadcasted_iota(jnp.int32, (nblk, s), 0)
    qblk = lax.broadcasted_iota(jnp.int32, (nblk, s), 1) // MOBA_BLOCK
    cand = jrow < qblk
    bias_rows = []
    for j in range(nblk):
        gj = gate[j:j + 1, :]
        beats = cand & ((gate > gj) | ((gate == gj) & (jrow < j)))
        rank = jnp.sum(beats.astype(F32), axis=0, keepdims=True)
        qb = qblk[0:1, :]
        keep = ((qb > j) & (rank < MOBA_TOPK)) | (qb == j)
        bias_rows.append(jnp.where(keep, 0.0, MASK_VALUE))
    bias = jnp.concatenate(bias_rows + [jnp.zeros((BF16_ROWS - nblk, s), F32)], axis=0)
    qa_ref[pl.ds(aux_lo, BF16_ROWS), :] = bias.astype(BF16)

    kr = lax.broadcasted_iota(jnp.int32, (MOBA_BLOCK, MOBA_BLOCK), 0)
    qc = lax.broadcasted_iota(jnp.int32, (MOBA_BLOCK, MOBA_BLOCK), 1)
    causal = kr <= qc

    def qblock(i, carry):
        c0 = pl.multiple_of(i * MOBA_BLOCK, MOBA_BLOCK)
        qa = qa_ref[:, pl.ds(c0, MOBA_BLOCK)]
        sc = jnp.dot(ka_ref[pl.ds(c0, MOBA_BLOCK), :], qa, preferred_element_type=F32)
        sc = jnp.where(causal, sc, MASK_VALUE)
        m = jnp.max(sc, axis=0, keepdims=True)
        p = jnp.exp(sc - m)
        l = jnp.sum(p, axis=0, keepdims=True)
        acc = jnp.dot(vt_ref[0, :, pl.ds(c0, MOBA_BLOCK)], p.astype(BF16),
                      preferred_element_type=F32)

        def kblock(j, st):
            m, l, acc = st
            r0 = pl.multiple_of(j * MOBA_BLOCK, MOBA_BLOCK)
            sc = jnp.dot(ka_ref[pl.ds(r0, MOBA_BLOCK), :], qa, preferred_element_type=F32)
            m_new = jnp.maximum(m, jnp.max(sc, axis=0, keepdims=True))
            alpha = jnp.exp(m - m_new)
            p = jnp.exp(sc - m_new)
            l = alpha * l + jnp.sum(p, axis=0, keepdims=True)
            acc = alpha * acc + jnp.dot(vt_ref[0, :, pl.ds(r0, MOBA_BLOCK)], p.astype(BF16),
                                        preferred_element_type=F32)
            return m_new, l, acc

        m, l, acc = lax.fori_loop(0, i, kblock, (m, l, acc))
        o_ref[0, :, pl.ds(c0, MOBA_BLOCK)] = (acc / l).astype(BF16)
        return carry

    lax.fori_loop(0, nblk, qblock, 0)


def _attention(qt, k, vt):
    bsz, w, s = qt.shape
    return pl.pallas_call(
        _attn_kernel,
        grid=(bsz, ATTN_HEADS),
        in_specs=[
            pl.BlockSpec((1, HEAD_DIM, s), lambda b, h: (b, h, 0)),
            pl.BlockSpec((1, s, LANES), lambda b, h: (b, 0, h // 2)),
            pl.BlockSpec((1, HEAD_DIM, s), lambda b, h: (b, h, 0)),
        ],
        out_specs=pl.BlockSpec((1, HEAD_DIM, s), lambda b, h: (b, h, 0)),
        out_shape=jax.ShapeDtypeStruct((bsz, w, s), BF16),
        scratch_shapes=[pltpu.VMEM((s, LANES), BF16), pltpu.VMEM((LANES, s), BF16)],
        compiler_params=pltpu.CompilerParams(
            dimension_semantics=("arbitrary", "arbitrary"), vmem_limit_bytes=VMEM_LIMIT),
        name="moba_attention",
    )(qt, k, vt)


LRU_CHUNK = 256


def _lru_kernel(xl_ref, zl_ref, cw_ref, cb_ref, wr_ref, wi_ref, br_ref, bi_ref, lam_ref,
                gain_ref, y_ref, ext_ref, a_ref, u_ref):
    s = xl_ref.shape[1]
    w = xl_ref.shape[2]
    nchunk = s // LRU_CHUNK
    pad = SUBLANES

    ext_ref[0:pad, :] = jnp.zeros((pad, w), F32)

    def fill(c, carry):
        r0 = pl.multiple_of(c * LRU_CHUNK, LRU_CHUNK)
        ext_ref[pl.ds(r0 + pad, LRU_CHUNK), :] = xl_ref[0, pl.ds(r0, LRU_CHUNK), :].astype(F32)
        return carry

    lax.fori_loop(0, nchunk, fill, 0)

    lam = lam_ref[...]
    neg_lam = -lam
    softplus = jnp.maximum(neg_lam, 0.0) + jnp.log1p(jnp.exp(-jnp.abs(neg_lam)))
    decay = -LRU_C * softplus
    rowi = lax.broadcasted_iota(jnp.int32, (LRU_CHUNK, w), 0) % SUBLANES

    def gates(c, carry):
        r0 = pl.multiple_of(c * LRU_CHUNK, LRU_CHUNK)
        e = ext_ref[pl.ds(r0, LRU_CHUNK + pad), :]
        xc = cw_ref[CONV_WIDTH - 1:CONV_WIDTH, :] * e
        for tap in range(1, CONV_WIDTH):
            xc = xc + cw_ref[CONV_WIDTH - 1 - tap:CONV_WIDTH - tap, :] * pltpu.roll(e, tap, axis=0)
        xc = xc[pad:] + cb_ref[...]
        xcb = xc.astype(BF16)
        r = _sigmoid(jnp.dot(xcb, wr_ref[...], preferred_element_type=F32) + br_ref[...])
        ig = _sigmoid(jnp.dot(xcb, wi_ref[...], preferred_element_type=F32) + bi_ref[...])
        a = jnp.exp(decay * r)
        u = jnp.sqrt(1.0 - a * a) * ig * xc
        for sh in (1, 2, 4):
            ok = rowi >= sh
            a_prev = pltpu.roll(a, sh, axis=0)
            u_prev = pltpu.roll(u, sh, axis=0)
            u = jnp.where(ok, a * u_prev + u, u)
            a = jnp.where(ok, a * a_prev, a)
        a_ref[pl.ds(r0, LRU_CHUNK), :] = a
        u_ref[pl.ds(r0, LRU_CHUNK), :] = u
        return carry

    lax.fori_loop(0, nchunk, gates, 0)

    def group(g, hprev):
        r0 = pl.multiple_of(g * SUBLANES, SUBLANES)
        hcur = u_ref[pl.ds(r0, SUBLANES), :] + a_ref[pl.ds(r0, SUBLANES), :] * hprev
        u_ref[pl.ds(r0, SUBLANES), :] = hcur
        return hcur[SUBLANES - 1:SUBLANES, :]

    lax.fori_loop(0, s // SUBLANES, group, jnp.zeros((1, w), F32), unroll=8)

    def finish(c, carry):
        r0 = pl.multiple_of(c * LRU_CHUNK, LRU_CHUNK)
        rec = u_ref[pl.ds(r0, LRU_CHUNK), :]
        rs = lax.rsqrt(jnp.mean(rec * rec, axis=-1, keepdims=True) + EPS)
        z = zl_ref[0, pl.ds(r0, LRU_CHUNK), :].astype(F32)
        y_ref[0, pl.ds(r0, LRU_CHUNK), :] = ((rec * rs) * gain_ref[...] * (z * _sigmoid(z))).astype(BF16)
        return carry

    lax.fori_loop(0, nchunk, finish, 0)


def _rglru(xl, zl, conv_w, conv_b, wr_bd, wi_bd, b_r, b_i, lam, gain):
    bsz, s, w = xl.shape
    row = lambda a: a.reshape(1, w)
    seq_spec = pl.BlockSpec((1, s, w), lambda b: (b, 0, 0))
    vec_spec = pl.BlockSpec((1, w), lambda b: (0, 0))
    return pl.pallas_call(
        _lru_kernel,
        grid=(bsz,),
        in_specs=[seq_spec, seq_spec,
                  pl.BlockSpec((CONV_WIDTH, w), lambda b: (0, 0)), vec_spec,
                  pl.BlockSpec((w, w), lambda b: (0, 0)), pl.BlockSpec((w, w), lambda b: (0, 0)),
                  vec_spec, vec_spec, vec_spec, vec_spec],
        out_specs=seq_spec,
        out_shape=jax.ShapeDtypeStruct((bsz, s, w), BF16),
        scratch_shapes=[pltpu.VMEM((s + SUBLANES, w), F32),
                        pltpu.VMEM((s, w), F32), pltpu.VMEM((s, w), F32)],
        compiler_params=pltpu.CompilerParams(
            dimension_semantics=("arbitrary",), vmem_limit_bytes=VMEM_LIMIT),
        name="rglru",
    )(xl, zl, conv_w, row(conv_b), wr_bd, wi_bd, row(b_r), row(b_i), row(lam), row(gain))


def _outproj_kernel(at_ref, zat_ref, yl_ref, x_ref, mod_ref, ga_ref, wo_ref, fg_ref, o_ref,
                    *, final_norm):
    d = x_ref.shape[2]
    w = ATTN_WIDTH
    at = at_ref[0].astype(F32)
    rs = lax.rsqrt(jnp.mean(at * at, axis=0, keepdims=True) + EPS)
    z = zat_ref[0].astype(F32)
    ya = ((at * rs) * ga_ref[...] * (z * _sigmoid(z))).astype(BF16)
    tn = (((0,), (0,)), ((), ()))
    y = lax.dot_general(ya, wo_ref[0:w, :], tn, preferred_element_type=F32)
    y = y + jnp.dot(yl_ref[0], wo_ref[w:2 * w, :], preferred_element_type=F32)
    gate = mod_ref[0, :, 2 * d:3 * d]
    xo = x_ref[0] + gate * y
    if final_norm:
        xo = (xo * lax.rsqrt(jnp.mean(xo * xo, axis=-1, keepdims=True) + EPS)) * fg_ref[...]
    o_ref[0] = xo


def _outproj(at, zat, yl, x, mod3, gain_a, w_out, final_gain, ts, final_norm):
    bsz, s, d = x.shape
    w = ATTN_WIDTH
    return pl.pallas_call(
        functools.partial(_outproj_kernel, final_norm=final_norm),
        grid=(bsz, s // ts),
        in_specs=[
            pl.BlockSpec((1, w, ts), lambda b, i: (b, 0, i)),
            pl.BlockSpec((1, w, ts), lambda b, i: (b, 0, i)),
            pl.BlockSpec((1, ts, w), lambda b, i: (b, i, 0)),
            pl.BlockSpec((1, ts, d), lambda b, i: (b, i, 0)),
            pl.BlockSpec((1, 1, 3 * d), lambda b, i: (b, 0, 0)),
            pl.BlockSpec((w, 1), lambda b, i: (0, 0)),
            pl.BlockSpec((2 * w, d), lambda b, i: (0, 0)),
            pl.BlockSpec((1, d), lambda b, i: (0, 0)),
        ],
        out_specs=pl.BlockSpec((1, ts, d), lambda b, i: (b, i, 0)),
        out_shape=jax.ShapeDtypeStruct((bsz, s, d), F32),
        compiler_params=pltpu.CompilerParams(
            dimension_semantics=("arbitrary", "arbitrary"), vmem_limit_bytes=VMEM_LIMIT),
        name="outproj",
    )(at, zat, yl, x, mod3, gain_a, w_out, final_gain)


def _rope_tables(s):
    pos = jnp.arange(s, dtype=F32)
    inv_freq = ROPE_THETA ** (-jnp.arange(0, HEAD_DIM, 2, dtype=F32) / HEAD_DIM)
    ang = pos[:, None] * inv_freq[None, :]
    cos, sin = jnp.cos(ang), jnp.sin(ang)
    cosn = jnp.tile(cos, (1, LANES // HALF_DIM))
    sinn = jnp.tile(jnp.concatenate([-sin, sin], axis=1), (1, LANES // HEAD_DIM))
    cost = jnp.concatenate([cos.T, cos.T], axis=0)
    sint = jnp.concatenate([-sin.T, sin.T], axis=0)
    return cosn, sinn, cost, sint


def _block_diag(wg):
    g, n, _ = wg.shape
    eye = jnp.eye(g, dtype=wg.dtype)
    return jnp.einsum("gij,gh->gihj", wg, eye).reshape(g * n, g * n)


def kernel(x, c, w_mod, b_mod, norm_gain, w_in, conv_w, conv_b, w_rgate, b_rgate,
           w_igate, b_igate, lru_lambda, attn_out_gain, lru_out_gain, w_out, final_gain):
    bsz, s, d = x.shape
    depth = w_in.shape[0]
    w = ATTN_WIDTH
    assert s % 512 == 0 and d % LANES == 0 and w_in.shape[2] == 6 * w
    ts = 512
    cosn, sinn, cost, sint = _rope_tables(s)
    for l in range(depth):
        mod3 = _modulation(c, w_mod[l], b_mod[l]).reshape(bsz, 1, 3 * d)
        wl = w_in[l]
        wq, wk, wv, wza, wxl, wzl = [wl[:, i * w:(i + 1) * w] for i in range(6)]
        w_nat = jnp.concatenate([wk, wxl, wzl], axis=1).astype(BF16)
        w_t = jnp.concatenate([wq, wv, wza], axis=1).T.astype(BF16)
        qt, vt, zat, k, xl, zl = _inproj(x, mod3, norm_gain[l].reshape(1, d), w_nat, w_t,
                                         cosn, sinn, cost, sint, ts)
        at = _attention(qt, k, vt)
        yl = _rglru(xl, zl, conv_w[l], conv_b[l],
                    _block_diag(w_rgate[l]).astype(BF16), _block_diag(w_igate[l]).astype(BF16),
                    b_rgate[l].reshape(-1), b_igate[l].reshape(-1), lru_lambda[l], lru_out_gain[l])
        x = _outproj(at, zat, yl, x, mod3, attn_out_gain[l].reshape(w, 1),
                     w_out[l].astype(BF16), final_gain.reshape(1, d), ts,
                     final_norm=(l == depth - 1))
    return x
```

```python
import functools

import jax
import jax.numpy as jnp
from jax import lax
from jax.experimental import pallas as pl
from jax.experimental.pallas import tpu as pltpu

ATTN_HEADS = 8
HEAD_DIM = 64
HALF_DIM = HEAD_DIM // 2
ATTN_WIDTH = ATTN_HEADS * HEAD_DIM
LRU_BLOCKS = 8
MOBA_BLOCK = 256
MOBA_TOPK = 3
CONV_WIDTH = 4
LRU_C = 8.0
ROPE_THETA = 10000.0
EPS = 1e-6

LANES = 128
SUBLANES = 8
BF16_ROWS = 16
MASK_VALUE = -1e30
VMEM_LIMIT = 56 * 1024 * 1024

F32 = jnp.float32
BF16 = jnp.bfloat16


def _sigmoid(z):
    return 1.0 / (1.0 + jnp.exp(-z))


def _split3(a):
    hi = a.astype(BF16)
    r1 = a - hi.astype(F32)
    mid = r1.astype(BF16)
    lo = (r1 - mid.astype(F32)).astype(BF16)
    return hi, mid, lo


def _mod_kernel(c_ref, w_ref, b_ref, o_ref):
    c = c_ref[...]
    act = c * _sigmoid(c)
    a_hi, a_mid, _ = _split3(act)
    w_hi, w_mid, _ = _split3(w_ref[...])
    dot = functools.partial(jnp.dot, preferred_element_type=F32)
    acc = dot(a_hi, w_hi) + dot(a_hi, w_mid) + dot(a_mid, w_hi)
    o_ref[...] = acc + b_ref[...]


def _modulation(c, w_mod, b_mod):
    bsz, d = c.shape
    n = w_mod.shape[1]
    tn = 512
    return pl.pallas_call(
        _mod_kernel,
        grid=(n // tn,),
        in_specs=[
            pl.BlockSpec((bsz, d), lambda j: (0, 0)),
            pl.BlockSpec((d, tn), lambda j: (0, j)),
            pl.BlockSpec((1, tn), lambda j: (0, j)),
        ],
        out_specs=pl.BlockSpec((bsz, tn), lambda j: (0, j)),
        out_shape=jax.ShapeDtypeStruct((bsz, n), F32),
        compiler_params=pltpu.CompilerParams(
            dimension_semantics=("arbitrary",), vmem_limit_bytes=VMEM_LIMIT),
        name="modulation",
    )(c, w_mod, b_mod.reshape(1, n))


def _inproj_kernel(x_ref, mod_ref, gain_ref, wn_ref, wt_ref, cosn_ref, sinn_ref,
                   cost_ref, sint_ref,
                   qt_ref, vt_ref, zat_ref, k_ref, xl_ref, zl_ref):
    d = x_ref.shape[2]
    w = ATTN_WIDTH
    x = x_ref[0]
    shift = mod_ref[0, :, 0:d]
    scale = mod_ref[0, :, d:2 * d]
    rs = lax.rsqrt(jnp.mean(x * x, axis=-1, keepdims=True) + EPS)
    h = (x * rs) * (gain_ref[...] * (1.0 + scale)) + shift
    hb = h.astype(BF16)

    k = jnp.dot(hb, wn_ref[:, 0:w], preferred_element_type=F32)
    cosn = cosn_ref[...]
    sinn = sinn_ref[...]
    lane = lax.broadcasted_iota(jnp.int32, cosn.shape, 1)
    first_half = (lane % HEAD_DIM) < HALF_DIM
    for g in range(w // LANES):
        kg = k[:, g * LANES:(g + 1) * LANES]
        swapped = jnp.where(first_half,
                            pltpu.roll(kg, LANES - HALF_DIM, axis=1),
                            pltpu.roll(kg, HALF_DIM, axis=1))
        k_ref[0, :, g * LANES:(g + 1) * LANES] = (kg * cosn + swapped * sinn).astype(BF16)
    xl_ref[0] = jnp.dot(hb, wn_ref[:, w:2 * w], preferred_element_type=F32).astype(BF16)
    zl_ref[0] = jnp.dot(hb, wn_ref[:, 2 * w:3 * w], preferred_element_type=F32).astype(BF16)

    nt = (((1,), (1,)), ((), ()))
    qt = lax.dot_general(wt_ref[0:w, :], hb, nt, preferred_element_type=F32)
    cost = cost_ref[...]
    sint = sint_ref[...]
    qscale = HEAD_DIM ** -0.5
    for hd in range(ATTN_HEADS):
        t = qt[hd * HEAD_DIM:(hd + 1) * HEAD_DIM]
        swapped = jnp.concatenate([t[HALF_DIM:], t[:HALF_DIM]], axis=0)
        qt_ref[0, hd * HEAD_DIM:(hd + 1) * HEAD_DIM, :] = (
            (t * cost + swapped * sint) * qscale).astype(BF16)
    vt_ref[0] = lax.dot_general(wt_ref[w:2 * w, :], hb, nt,
                                preferred_element_type=F32).astype(BF16)
    zat_ref[0] = lax.dot_general(wt_ref[2 * w:3 * w, :], hb, nt,
                                 preferred_element_type=F32).astype(BF16)


def _inproj(x, mod3, gain, w_nat, w_t, cosn, sinn, cost, sint, ts):
    bsz, s, d = x.shape
    w = ATTN_WIDTH
    fm = jax.ShapeDtypeStruct((bsz, w, s), BF16)
    sm = jax.ShapeDtypeStruct((bsz, s, w), BF16)
    fm_spec = pl.BlockSpec((1, w, ts), lambda b, i: (b, 0, i))
    sm_spec = pl.BlockSpec((1, ts, w), lambda b, i: (b, i, 0))
    return pl.pallas_call(
        _inproj_kernel,
        grid=(bsz, s // ts),
        in_specs=[
            pl.BlockSpec((1, ts, d), lambda b, i: (b, i, 0)),
            pl.BlockSpec((1, 1, 3 * d), lambda b, i: (b, 0, 0)),
            pl.BlockSpec((1, d), lambda b, i: (0, 0)),
            pl.BlockSpec((d, 3 * w), lambda b, i: (0, 0)),
            pl.BlockSpec((3 * w, d), lambda b, i: (0, 0)),
            pl.BlockSpec((ts, LANES), lambda b, i: (i, 0)),
            pl.BlockSpec((ts, LANES), lambda b, i: (i, 0)),
            pl.BlockSpec((HEAD_DIM, ts), lambda b, i: (0, i)),
            pl.BlockSpec((HEAD_DIM, ts), lambda b, i: (0, i)),
        ],
        out_specs=[fm_spec, fm_spec, fm_spec, sm_spec, sm_spec, sm_spec],
        out_shape=[fm, fm, fm, sm, sm, sm],
        compiler_params=pltpu.CompilerParams(
            dimension_semantics=("arbitrary", "arbitrary"), vmem_limit_bytes=VMEM_LIMIT),
        name="inproj",
    )(x, mod3, gain, w_nat, w_t, cosn, sinn, cost, sint)


HEADS_PER_STEP = LANES // HEAD_DIM
BLOCK_SHIFT = MOBA_BLOCK.bit_length() - 1
assert 1 << BLOCK_SHIFT == MOBA_BLOCK


def _attn_kernel(qt_ref, k_ref, vt_ref, oh_ref, avg_ref, o_ref, ka_ref, qa_ref, s_ref, p_ref):
    s = qt_ref.shape[2]
    nblk = s // MOBA_BLOCK
    kp = k_ref[0]

    kmean = jnp.dot(avg_ref[...], kp, preferred_element_type=F32)
    km3 = jnp.concatenate(_split3(kmean), axis=0)

    lane1 = lax.broadcasted_iota(jnp.int32, (1, LANES), 1)
    jrow = lax.broadcasted_iota(jnp.int32, (nblk, s), 0)
    qblk = lax.shift_right_logical(lax.broadcasted_iota(jnp.int32, (nblk, s), 1), BLOCK_SHIFT)
    cand = jrow < qblk
    qb = qblk[0:1, :]
    kr = lax.broadcasted_iota(jnp.int32, (MOBA_BLOCK, MOBA_BLOCK), 0)
    qc = lax.broadcasted_iota(jnp.int32, (MOBA_BLOCK, MOBA_BLOCK), 1)
    causal = kr <= qc

    for hh in range(HEADS_PER_STEP):
        head_lo = hh * HEAD_DIM
        aux_lo = HEAD_DIM - head_lo
        rows_h = slice(head_lo, head_lo + HEAD_DIM)

        hmask = ((lane1 >= head_lo) & (lane1 < head_lo + HEAD_DIM)).astype(BF16)
        ka_ref[hh] = kp * hmask + oh_ref[hh]

        qa_ref[hh] = jnp.zeros(qa_ref.shape[1:], BF16)
        qa_ref[hh, rows_h, :] = qt_ref[0, rows_h, :]
        g3 = jnp.dot(km3, qa_ref[hh], preferred_element_type=F32)
        gate = (g3[0:BF16_ROWS] + g3[BF16_ROWS:2 * BF16_ROWS]) + g3[2 * BF16_ROWS:]
        gate = gate[0:nblk]

        bias_rows = []
        for j in range(nblk):
            gj = gate[j:j + 1, :]
            beats = cand & ((gate > gj) | ((gate == gj) & (jrow < j)))
            rank = jnp.sum(beats.astype(F32), axis=0, keepdims=True)
            keep = ((qb > j) & (rank < MOBA_TOPK)) | (qb == j)
            bias_rows.append(jnp.where(keep, 0.0, MASK_VALUE))
        bias = jnp.concatenate(bias_rows + [jnp.zeros((BF16_ROWS - nblk, s), F32)], axis=0)
        qa_ref[hh, aux_lo:aux_lo + BF16_ROWS, :] = bias.astype(BF16)

        for i in range(nblk):
            buf = (hh * nblk + i) % 2
            past = i * MOBA_BLOCK
            rows = past + MOBA_BLOCK
            cols = slice(past, rows)
            sc = jnp.dot(ka_ref[hh, 0:rows, :], qa_ref[hh, :, cols],
                         preferred_element_type=F32)
            diag = jnp.where(causal, sc[past:rows], MASK_VALUE)
            s_ref[buf, past:rows, :] = diag
            m = jnp.max(diag, axis=0, keepdims=True)
            if i > 0:
                s_ref[buf, 0:past, :] = sc[0:past]
                m = jnp.maximum(m, jnp.max(sc[0:past], axis=0, keepdims=True))
            p = jnp.exp(s_ref[buf, 0:rows, :] - m)
            l = jnp.sum(p, axis=0, keepdims=True)
            p_ref[buf, 0:rows, :] = p.astype(BF16)
            acc = jnp.dot(vt_ref[0, rows_h, 0:rows], p_ref[buf, 0:rows, :],
                          preferred_element_type=F32)
            o_ref[0, rows_h, cols] = (acc / l).astype(BF16)


def _attention(qt, k, vt):
    bsz, w, s = qt.shape
    blk = jnp.arange(s, dtype=jnp.int32) // MOBA_BLOCK
    lane = jnp.arange(LANES, dtype=jnp.int32)
    oh = jnp.stack([(lane[None, :] == (HEAD_DIM - hh * HEAD_DIM) + blk[:, None])
                    for hh in range(HEADS_PER_STEP)]).astype(BF16)
    avg = jnp.where(jnp.arange(BF16_ROWS, dtype=jnp.int32)[:, None] == blk[None, :],
                    1.0 / MOBA_BLOCK, 0.0).astype(BF16)
    pair_fm = pl.BlockSpec((1, LANES, s), lambda b, p: (b, p, 0))
    return pl.pallas_call(
        _attn_kernel,
        grid=(bsz, ATTN_HEADS // HEADS_PER_STEP),
        in_specs=[
            pair_fm,
            pl.BlockSpec((1, s, LANES), lambda b, p: (b, 0, p)),
            pair_fm,
            pl.BlockSpec((HEADS_PER_STEP, s, LANES), lambda b, p: (0, 0, 0)),
            pl.BlockSpec((BF16_ROWS, s), lambda b, p: (0, 0)),
        ],
        out_specs=pair_fm,
        out_shape=jax.ShapeDtypeStruct((bsz, w, s), BF16),
        scratch_shapes=[pltpu.VMEM((HEADS_PER_STEP, s, LANES), BF16),
                        pltpu.VMEM((HEADS_PER_STEP, LANES, s), BF16),
                        pltpu.VMEM((2, s, MOBA_BLOCK), F32),
                        pltpu.VMEM((2, s, MOBA_BLOCK), BF16)],
        compiler_params=pltpu.CompilerParams(
            dimension_semantics=("arbitrary", "arbitrary"), vmem_limit_bytes=VMEM_LIMIT),
        name="moba_attention",
    )(qt, k, vt, oh, avg)


LRU_CHUNK = 256


def _lru_kernel(xl_ref, zl_ref, cw_ref, cb_ref, wr_ref, wi_ref, br_ref, bi_ref, lam_ref,
                gain_ref, y_ref, ext_ref, a_ref, u_ref):
    s = xl_ref.shape[1]
    w = xl_ref.shape[2]
    nchunk = s // LRU_CHUNK
    pad = SUBLANES

    ext_ref[0:pad, :] = jnp.zeros((pad, w), F32)

    def fill(c, carry):
        r0 = pl.multiple_of(c * LRU_CHUNK, LRU_CHUNK)
        ext_ref[pl.ds(r0 + pad, LRU_CHUNK), :] = xl_ref[0, pl.ds(r0, LRU_CHUNK), :].astype(F32)
        return carry

    lax.fori_loop(0, nchunk, fill, 0)

    lam = lam_ref[...]
    neg_lam = -lam
    softplus = jnp.maximum(neg_lam, 0.0) + jnp.log1p(jnp.exp(-jnp.abs(neg_lam)))
    decay = -LRU_C * softplus
    rowi = lax.broadcasted_iota(jnp.int32, (LRU_CHUNK, w), 0) % SUBLANES

    def gates(c, carry):
        r0 = pl.multiple_of(c * LRU_CHUNK, LRU_CHUNK)
        e = ext_ref[pl.ds(r0, LRU_CHUNK + pad), :]
        xc = cw_ref[CONV_WIDTH - 1:CONV_WIDTH, :] * e
        for tap in range(1, CONV_WIDTH):
            xc = xc + cw_ref[CONV_WIDTH - 1 - tap:CONV_WIDTH - tap, :] * pltpu.roll(e, tap, axis=0)
        xc = xc[pad:] + cb_ref[...]
        xcb = xc.astype(BF16)
        r = _sigmoid(jnp.dot(xcb, wr_ref[...], preferred_element_type=F32) + br_ref[...])
        ig = _sigmoid(jnp.dot(xcb, wi_ref[...], preferred_element_type=F32) + bi_ref[...])
        a = jnp.exp(decay * r)
        u = jnp.sqrt(1.0 - a * a) * ig * xc
        for sh in (1, 2, 4):
            ok = rowi >= sh
            a_prev = pltpu.roll(a, sh, axis=0)
            u_prev = pltpu.roll(u, sh, axis=0)
            u = jnp.where(ok, a * u_prev + u, u)
            a = jnp.where(ok, a * a_prev, a)
        a_ref[pl.ds(r0, LRU_CHUNK), :] = a
        u_ref[pl.ds(r0, LRU_CHUNK), :] = u
        return carry

    lax.fori_loop(0, nchunk, gates, 0)

    def group(g, hprev):
        r0 = pl.multiple_of(g * SUBLANES, SUBLANES)
        hcur = u_ref[pl.ds(r0, SUBLANES), :] + a_ref[pl.ds(r0, SUBLANES), :] * hprev
        u_ref[pl.ds(r0, SUBLANES), :] = hcur
        return hcur[SUBLANES - 1:SUBLANES, :]

    lax.fori_loop(0, s // SUBLANES, group, jnp.zeros((1, w), F32), unroll=8)

    def finish(c, carry):
        r0 = pl.multiple_of(c * LRU_CHUNK, LRU_CHUNK)
        rec = u_ref[pl.ds(r0, LRU_CHUNK), :]
        rs = lax.rsqrt(jnp.mean(rec * rec, axis=-1, keepdims=True) + EPS)
        z = zl_ref[0, pl.ds(r0, LRU_CHUNK), :].astype(F32)
        y_ref[0, pl.ds(r0, LRU_CHUNK), :] = ((rec * rs) * gain_ref[...] * (z * _sigmoid(z))).astype(BF16)
        return carry

    lax.fori_loop(0, nchunk, finish, 0)


def _rglru(xl, zl, conv_w, conv_b, wr_bd, wi_bd, b_r, b_i, lam, gain):
    bsz, s, w = xl.shape
    row = lambda a: a.reshape(1, w)
    seq_spec = pl.BlockSpec((1, s, w), lambda b: (b, 0, 0))
    vec_spec = pl.BlockSpec((1, w), lambda b: (0, 0))
    return pl.pallas_call(
        _lru_kernel,
        grid=(bsz,),
        in_specs=[seq_spec, seq_spec,
                  pl.BlockSpec((CONV_WIDTH, w), lambda b: (0, 0)), vec_spec,
                  pl.BlockSpec((w, w), lambda b: (0, 0)), pl.BlockSpec((w, w), lambda b: (0, 0)),
                  vec_spec, vec_spec, vec_spec, vec_spec],
        out_specs=seq_spec,
        out_shape=jax.ShapeDtypeStruct((bsz, s, w), BF16),
        scratch_shapes=[pltpu.VMEM((s + SUBLANES, w), F32),
                        pltpu.VMEM((s, w), F32), pltpu.VMEM((s, w), F32)],
        compiler_params=pltpu.CompilerParams(
            dimension_semantics=("arbitrary",), vmem_limit_bytes=VMEM_LIMIT),
        name="rglru",
    )(xl, zl, conv_w, row(conv_b), wr_bd, wi_bd, row(b_r), row(b_i), row(lam), row(gain))


def _outproj_kernel(at_ref, zat_ref, yl_ref, x_ref, mod_ref, ga_ref, wo_ref, fg_ref, o_ref,
                    *, final_norm):
    d = x_ref.shape[2]
    w = ATTN_WIDTH
    at = at_ref[0].astype(F32)
    rs = lax.rsqrt(jnp.mean(at * at, axis=0, keepdims=True) + EPS)
    z = zat_ref[0].astype(F32)
    ya = ((at * rs) * ga_ref[...] * (z * _sigmoid(z))).astype(BF16)
    tn = (((0,), (0,)), ((), ()))
    y = lax.dot_general(ya, wo_ref[0:w, :], tn, preferred_element_type=F32)
    y = y + jnp.dot(yl_ref[0], wo_ref[w:2 * w, :], preferred_element_type=F32)
    gate = mod_ref[0, :, 2 * d:3 * d]
    xo = x_ref[0] + gate * y
    if final_norm:
        xo = (xo * lax.rsqrt(jnp.mean(xo * xo, axis=-1, keepdims=True) + EPS)) * fg_ref[...]
    o_ref[0] = xo


def _outproj(at, zat, yl, x, mod3, gain_a, w_out, final_gain, ts, final_norm):
    bsz, s, d = x.shape
    w = ATTN_WIDTH
    return pl.pallas_call(
        functools.partial(_outproj_kernel, final_norm=final_norm),
        grid=(bsz, s // ts),
        in_specs=[
            pl.BlockSpec((1, w, ts), lambda b, i: (b, 0, i)),
            pl.BlockSpec((1, w, ts), lambda b, i: (b, 0, i)),
            pl.BlockSpec((1, ts, w), lambda b, i: (b, i, 0)),
            pl.BlockSpec((1, ts, d), lambda b, i: (b, i, 0)),
            pl.BlockSpec((1, 1, 3 * d), lambda b, i: (b, 0, 0)),
            pl.BlockSpec((w, 1), lambda b, i: (0, 0)),
            pl.BlockSpec((2 * w, d), lambda b, i: (0, 0)),
            pl.BlockSpec((1, d), lambda b, i: (0, 0)),
        ],
        out_specs=pl.BlockSpec((1, ts, d), lambda b, i: (b, i, 0)),
        out_shape=jax.ShapeDtypeStruct((bsz, s, d), F32),
        compiler_params=pltpu.CompilerParams(
            dimension_semantics=("arbitrary", "arbitrary"), vmem_limit_bytes=VMEM_LIMIT),
        name="outproj",
    )(at, zat, yl, x, mod3, gain_a, w_out, final_gain)


def _rope_tables(s):
    pos = jnp.arange(s, dtype=F32)
    inv_freq = ROPE_THETA ** (-jnp.arange(0, HEAD_DIM, 2, dtype=F32) / HEAD_DIM)
    ang = pos[:, None] * inv_freq[None, :]
    cos, sin = jnp.cos(ang), jnp.sin(ang)
    cosn = jnp.tile(cos, (1, LANES // HALF_DIM))
    sinn = jnp.tile(jnp.concatenate([-sin, sin], axis=1), (1, LANES // HEAD_DIM))
    cost = jnp.concatenate([cos.T, cos.T], axis=0)
    sint = jnp.concatenate([-sin.T, sin.T], axis=0)
    return cosn, sinn, cost, sint


def _block_diag(wg):
    g, n, _ = wg.shape
    eye = jnp.eye(g, dtype=wg.dtype)
    return jnp.einsum("gij,gh->gihj", wg, eye).reshape(g * n, g * n)


def kernel(x, c, w_mod, b_mod, norm_gain, w_in, conv_w, conv_b, w_rgate, b_rgate,
           w_igate, b_igate, lru_lambda, attn_out_gain, lru_out_gain, w_out, final_gain):
    bsz, s, d = x.shape
    depth = w_in.shape[0]
    w = ATTN_WIDTH
    assert s % 512 == 0 and d % LANES == 0 and w_in.shape[2] == 6 * w
    ts = 512
    cosn, sinn, cost, sint = _rope_tables(s)
    for l in range(depth):
        mod3 = _modulation(c, w_mod[l], b_mod[l]).reshape(bsz, 1, 3 * d)
        wl = w_in[l]
        wq, wk, wv, wza, wxl, wzl = [wl[:, i * w:(i + 1) * w] for i in range(6)]
        w_nat = jnp.concatenate([wk, wxl, wzl], axis=1).astype(BF16)
        w_t = jnp.concatenate([wq, wv, wza], axis=1).T.astype(BF16)
        qt, vt, zat, k, xl, zl = _inproj(x, mod3, norm_gain[l].reshape(1, d), w_nat, w_t,
                                         cosn, sinn, cost, sint, ts)
        at = _attention(qt, k, vt)
        yl = _rglru(xl, zl, conv_w[l], conv_b[l],
                    _block_diag(w_rgate[l]).astype(BF16), _block_diag(w_igate[l]).astype(BF16),
                    b_rgate[l].reshape(-1), b_igate[l].reshape(-1), lru_lambda[l], lru_out_gain[l])
        x = _outproj(at, zat, yl, x, mod3, attn_out_gain[l].reshape(w, 1),
                     w_out[l].astype(BF16), final_gain.reshape(1, d), ts,
                     final_norm=(l == depth - 1))
    return x
```

```python
import functools

import jax
import jax.numpy as jnp
from jax import lax
from jax.experimental import pallas as pl
from jax.experimental.pallas import tpu as pltpu

ATTN_HEADS = 8
HEAD_DIM = 64
HALF_DIM = HEAD_DIM // 2
ATTN_WIDTH = ATTN_HEADS * HEAD_DIM
LRU_BLOCKS = 8
MOBA_BLOCK = 256
MOBA_TOPK = 3
CONV_WIDTH = 4
LRU_C = 8.0
ROPE_THETA = 10000.0
EPS = 1e-6

LANES = 128
SUBLANES = 8
BF16_ROWS = 16
MASK_VALUE = -1e30
LOG2_E = 1.4426950408889634
VMEM_LIMIT = 56 * 1024 * 1024

F32 = jnp.float32
BF16 = jnp.bfloat16


def _sigmoid(z):
    return 1.0 / (1.0 + jnp.exp(-z))


def _split3(a):
    hi = a.astype(BF16)
    r1 = a - hi.astype(F32)
    mid = r1.astype(BF16)
    lo = (r1 - mid.astype(F32)).astype(BF16)
    return hi, mid, lo


def _mod_kernel(c_ref, w_ref, b_ref, o_ref):
    c = c_ref[...]
    act = c * _sigmoid(c)
    a_hi, a_mid, _ = _split3(act)
    w_hi, w_mid, _ = _split3(w_ref[...])
    dot = functools.partial(jnp.dot, preferred_element_type=F32)
    acc = dot(a_hi, w_hi) + dot(a_hi, w_mid) + dot(a_mid, w_hi)
    o_ref[...] = acc + b_ref[...]


def _modulation(c, w_mod, b_mod):
    bsz, d = c.shape
    n = w_mod.shape[1]
    tn = 512
    return pl.pallas_call(
        _mod_kernel,
        grid=(n // tn,),
        in_specs=[
            pl.BlockSpec((bsz, d), lambda j: (0, 0)),
            pl.BlockSpec((d, tn), lambda j: (0, j)),
            pl.BlockSpec((1, tn), lambda j: (0, j)),
        ],
        out_specs=pl.BlockSpec((bsz, tn), lambda j: (0, j)),
        out_shape=jax.ShapeDtypeStruct((bsz, n), F32),
        compiler_params=pltpu.CompilerParams(
            dimension_semantics=("arbitrary",), vmem_limit_bytes=VMEM_LIMIT),
        name="modulation",
    )(c, w_mod, b_mod.reshape(1, n))


def _inproj_kernel(x_ref, mod_ref, gain_ref, wn_ref, wt_ref, cosn_ref, sinn_ref,
                   cost_ref, sint_ref,
                   qt_ref, vt_ref, zat_ref, k_ref, xl_ref, zl_ref):
    d = x_ref.shape[2]
    w = ATTN_WIDTH
    x = x_ref[0]
    shift = mod_ref[0, :, 0:d]
    scale = mod_ref[0, :, d:2 * d]
    rs = lax.rsqrt(jnp.mean(x * x, axis=-1, keepdims=True) + EPS)
    h = (x * rs) * (gain_ref[...] * (1.0 + scale)) + shift
    hb = h.astype(BF16)

    k = jnp.dot(hb, wn_ref[:, 0:w], preferred_element_type=F32)
    cosn = cosn_ref[...]
    sinn = sinn_ref[...]
    lane = lax.broadcasted_iota(jnp.int32, cosn.shape, 1)
    first_half = (lane % HEAD_DIM) < HALF_DIM
    for g in range(w // LANES):
        kg = k[:, g * LANES:(g + 1) * LANES]
        swapped = jnp.where(first_half,
                            pltpu.roll(kg, LANES - HALF_DIM, axis=1),
                            pltpu.roll(kg, HALF_DIM, axis=1))
        k_ref[0, :, g * LANES:(g + 1) * LANES] = (kg * cosn + swapped * sinn).astype(BF16)
    xl_ref[0] = jnp.dot(hb, wn_ref[:, w:2 * w], preferred_element_type=F32).astype(BF16)
    zl_ref[0] = jnp.dot(hb, wn_ref[:, 2 * w:3 * w], preferred_element_type=F32).astype(BF16)

    nt = (((1,), (1,)), ((), ()))
    qt = lax.dot_general(wt_ref[0:w, :], hb, nt, preferred_element_type=F32)
    cost = cost_ref[...]
    sint = sint_ref[...]
    qscale = HEAD_DIM ** -0.5 * LOG2_E
    for hd in range(ATTN_HEADS):
        t = qt[hd * HEAD_DIM:(hd + 1) * HEAD_DIM]
        swapped = jnp.concatenate([t[HALF_DIM:], t[:HALF_DIM]], axis=0)
        qt_ref[0, hd * HEAD_DIM:(hd + 1) * HEAD_DIM, :] = (
            (t * cost + swapped * sint) * qscale).astype(BF16)
    vt_ref[0] = lax.dot_general(wt_ref[w:2 * w, :], hb, nt,
                                preferred_element_type=F32).astype(BF16)
    zat_ref[0] = lax.dot_general(wt_ref[2 * w:3 * w, :], hb, nt,
                                 preferred_element_type=F32).astype(BF16)


def _inproj(x, mod3, gain, w_nat, w_t, cosn, sinn, cost, sint, ts):
    bsz, s, d = x.shape
    w = ATTN_WIDTH
    fm = jax.ShapeDtypeStruct((bsz, w, s), BF16)
    sm = jax.ShapeDtypeStruct((bsz, s, w), BF16)
    fm_spec = pl.BlockSpec((1, w, ts), lambda b, i: (b, 0, i))
    sm_spec = pl.BlockSpec((1, ts, w), lambda b, i: (b, i, 0))
    return pl.pallas_call(
        _inproj_kernel,
        grid=(bsz, s // ts),
        in_specs=[
            pl.BlockSpec((1, ts, d), lambda b, i: (b, i, 0)),
            pl.BlockSpec((1, 1, 3 * d), lambda b, i: (b, 0, 0)),
            pl.BlockSpec((1, d), lambda b, i: (0, 0)),
            pl.BlockSpec((d, 3 * w), lambda b, i: (0, 0)),
            pl.BlockSpec((3 * w, d), lambda b, i: (0, 0)),
            pl.BlockSpec((ts, LANES), lambda b, i: (i, 0)),
            pl.BlockSpec((ts, LANES), lambda b, i: (i, 0)),
            pl.BlockSpec((HEAD_DIM, ts), lambda b, i: (0, i)),
            pl.BlockSpec((HEAD_DIM, ts), lambda b, i: (0, i)),
        ],
        out_specs=[fm_spec, fm_spec, fm_spec, sm_spec, sm_spec, sm_spec],
        out_shape=[fm, fm, fm, sm, sm, sm],
        compiler_params=pltpu.CompilerParams(
            dimension_semantics=("arbitrary", "arbitrary"), vmem_limit_bytes=VMEM_LIMIT),
        name="inproj",
    )(x, mod3, gain, w_nat, w_t, cosn, sinn, cost, sint)


HEADS_PER_STEP = LANES // HEAD_DIM
BLOCK_SHIFT = MOBA_BLOCK.bit_length() - 1
assert 1 << BLOCK_SHIFT == MOBA_BLOCK
ATTN_LEAD = 2
ATTN_RING = 12


def _attn_kernel(qt_ref, k_ref, vt_ref, oh_ref, avg_ref, o_ref, ka_ref, qa_ref, va_ref, s_ref):
    s = qt_ref.shape[2]
    nblk = s // MOBA_BLOCK
    kp = k_ref[0]

    kmean = jnp.dot(avg_ref[...], kp, preferred_element_type=F32)
    km3 = jnp.concatenate(_split3(kmean), axis=0)

    jrow = lax.broadcasted_iota(jnp.int32, (nblk, s), 0)
    qblk = lax.shift_right_logical(lax.broadcasted_iota(jnp.int32, (nblk, s), 1), BLOCK_SHIFT)
    cand = jrow < qblk
    qb = qblk[0:1, :]
    kr = lax.broadcasted_iota(jnp.int32, (MOBA_BLOCK, MOBA_BLOCK), 0)
    qc = lax.broadcasted_iota(jnp.int32, (MOBA_BLOCK, MOBA_BLOCK), 1)
    causal = kr <= qc

    for hh in range(HEADS_PER_STEP):
        head_lo = hh * HEAD_DIM
        aux_lo = HEAD_DIM - head_lo
        rows_h = slice(head_lo, head_lo + HEAD_DIM)

        ka_ref[hh] = oh_ref[hh]
        ka_ref[hh, :, rows_h] = k_ref[0, :, rows_h]

        va_ref[hh, 0:HEAD_DIM, :] = vt_ref[0, rows_h, :]
        va_ref[hh, HEAD_DIM:, :] = jnp.ones((BF16_ROWS, s), BF16)

        qa_ref[hh] = jnp.zeros(qa_ref.shape[1:], BF16)
        qa_ref[hh, rows_h, :] = qt_ref[0, rows_h, :]
        g3 = jnp.dot(km3, qa_ref[hh], preferred_element_type=F32)
        gate = (g3[0:BF16_ROWS] + g3[BF16_ROWS:2 * BF16_ROWS]) + g3[2 * BF16_ROWS:]
        gate = gate[0:nblk]

        bias_rows = []
        for j in range(nblk):
            gj = gate[j:j + 1, :]
            beats = cand & ((gate > gj) | ((gate == gj) & (jrow < j)))
            rank = jnp.sum(beats.astype(F32), axis=0, keepdims=True)
            keep = ((qb > j) & (rank < MOBA_TOPK)) | (qb == j)
            bias_rows.append(jnp.where(keep, 0.0, MASK_VALUE))
        bias = jnp.concatenate(bias_rows + [jnp.zeros((BF16_ROWS - nblk, s), F32)], axis=0)
        qa_ref[hh, aux_lo:aux_lo + BF16_ROWS, :] = bias.astype(BF16)

    items = ([(i, 0) for i in range(nblk)] +
             [(i, hh) for hh in range(1, HEADS_PER_STEP) for i in reversed(range(nblk))])
    piece_list = [(n, j) for n, (i, _) in enumerate(items) for j in range(i + 1)]
    item_end = {}
    for t, (n, _) in enumerate(piece_list):
        item_end[n] = t + 1
    state = [dict() for _ in items]
    ring = s_ref.shape[0]
    slot0 = jnp.minimum(pl.program_id(1), 0)

    def blk_rows(j):
        return slice(j * MOBA_BLOCK, (j + 1) * MOBA_BLOCK)

    def score_piece(t):
        n, j = piece_list[t]
        i, hh = items[n]
        sc = jnp.dot(ka_ref[hh, blk_rows(j), :], qa_ref[hh, :, blk_rows(i)],
                     preferred_element_type=F32)
        if j == i:
            sc = jnp.where(causal, sc, MASK_VALUE)
        s_ref[slot0 + t % ring] = sc
        cm = jnp.max(sc, axis=0, keepdims=True)
        st = state[n]
        st["m"] = cm if j == 0 else jnp.maximum(st["m"], cm)

    def softmax_piece(t):
        n, j = piece_list[t]
        i, hh = items[n]
        rows_h = slice(hh * HEAD_DIM, (hh + 1) * HEAD_DIM)
        st = state[n]
        p = jnp.exp2(s_ref[slot0 + t % ring] - st["m"])
        pv = jnp.dot(va_ref[hh, :, blk_rows(j)], p.astype(BF16),
                     preferred_element_type=F32)
        st["acc"] = pv if j == 0 else st["acc"] + pv
        if j == i:
            acc = st["acc"]
            o_ref[0, rows_h, blk_rows(i)] = (
                acc[0:HEAD_DIM] / acc[HEAD_DIM:HEAD_DIM + 1]).astype(BF16)

    total = len(piece_list)
    scored = softmaxed = 0
    while softmaxed < total:
        if scored < total:
            assert scored - softmaxed < ring, "score ring too small for the softmax lag"
            score_piece(scored)
            scored += 1
        ready = min(total, item_end[piece_list[softmaxed][0]] + ATTN_LEAD)
        if scored >= ready:
            softmax_piece(softmaxed)
            softmaxed += 1


def _attention(qt, k, vt):
    bsz, w, s = qt.shape
    blk = jnp.arange(s, dtype=jnp.int32) // MOBA_BLOCK
    lane = jnp.arange(LANES, dtype=jnp.int32)
    oh = jnp.stack([(lane[None, :] == (HEAD_DIM - hh * HEAD_DIM) + blk[:, None])
                    for hh in range(HEADS_PER_STEP)]).astype(BF16)
    avg = jnp.where(jnp.arange(BF16_ROWS, dtype=jnp.int32)[:, None] == blk[None, :],
                    1.0 / MOBA_BLOCK, 0.0).astype(BF16)
    pair_fm = pl.BlockSpec((1, LANES, s), lambda b, p: (b, p, 0))
    return pl.pallas_call(
        _attn_kernel,
        grid=(bsz, ATTN_HEADS // HEADS_PER_STEP),
        in_specs=[
            pair_fm,
            pl.BlockSpec((1, s, LANES), lambda b, p: (b, 0, p)),
            pair_fm,
            pl.BlockSpec((HEADS_PER_STEP, s, LANES), lambda b, p: (0, 0, 0)),
            pl.BlockSpec((BF16_ROWS, s), lambda b, p: (0, 0)),
        ],
        out_specs=pair_fm,
        out_shape=jax.ShapeDtypeStruct((bsz, w, s), BF16),
        scratch_shapes=[pltpu.VMEM((HEADS_PER_STEP, s, LANES), BF16),
                        pltpu.VMEM((HEADS_PER_STEP, LANES, s), BF16),
                        pltpu.VMEM((HEADS_PER_STEP, HEAD_DIM + BF16_ROWS, s), BF16),
                        pltpu.VMEM((ATTN_RING, MOBA_BLOCK, MOBA_BLOCK), F32)],
        compiler_params=pltpu.CompilerParams(
            dimension_semantics=("arbitrary", "arbitrary"), vmem_limit_bytes=VMEM_LIMIT),
        name="moba_attention",
    )(qt, k, vt, oh, avg)


LRU_CHUNK = 256


def _lru_kernel(xl_ref, zl_ref, cw_ref, cb_ref, wr_ref, wi_ref, br_ref, bi_ref, lam_ref,
                gain_ref, y_ref, ext_ref, a_ref, u_ref):
    s = xl_ref.shape[1]
    w = xl_ref.shape[2]
    nchunk = s // LRU_CHUNK
    pad = SUBLANES

    ext_ref[0:pad, :] = jnp.zeros((pad, w), F32)

    def fill(c, carry):
        r0 = pl.multiple_of(c * LRU_CHUNK, LRU_CHUNK)
        ext_ref[pl.ds(r0 + pad, LRU_CHUNK), :] = xl_ref[0, pl.ds(r0, LRU_CHUNK), :].astype(F32)
        return carry

    lax.fori_loop(0, nchunk, fill, 0)

    lam = lam_ref[...]
    neg_lam = -lam
    softplus = jnp.maximum(neg_lam, 0.0) + jnp.log1p(jnp.exp(-jnp.abs(neg_lam)))
    decay = -LRU_C * softplus
    rowi = lax.broadcasted_iota(jnp.int32, (LRU_CHUNK, w), 0) % SUBLANES

    def gates(c, carry):
        r0 = pl.multiple_of(c * LRU_CHUNK, LRU_CHUNK)
        e = ext_ref[pl.ds(r0, LRU_CHUNK + pad), :]
        xc = cw_ref[CONV_WIDTH - 1:CONV_WIDTH, :] * e
        for tap in range(1, CONV_WIDTH):
            xc = xc + cw_ref[CONV_WIDTH - 1 - tap:CONV_WIDTH - tap, :] * pltpu.roll(e, tap, axis=0)
        xc = xc[pad:] + cb_ref[...]
        xcb = xc.astype(BF16)
        r = _sigmoid(jnp.dot(xcb, wr_ref[...], preferred_element_type=F32) + br_ref[...])
        ig = _sigmoid(jnp.dot(xcb, wi_ref[...], preferred_element_type=F32) + bi_ref[...])
        a = jnp.exp(decay * r)
        u = jnp.sqrt(1.0 - a * a) * ig * xc
        for sh in (1, 2, 4):
            ok = rowi >= sh
            a_prev = pltpu.roll(a, sh, axis=0)
            u_prev = pltpu.roll(u, sh, axis=0)
            u = jnp.where(ok, a * u_prev + u, u)
            a = jnp.where(ok, a * a_prev, a)
        a_ref[pl.ds(r0, LRU_CHUNK), :] = a
        u_ref[pl.ds(r0, LRU_CHUNK), :] = u
        return carry

    lax.fori_loop(0, nchunk, gates, 0)

    def group(g, hprev):
        r0 = pl.multiple_of(g * SUBLANES, SUBLANES)
        hcur = u_ref[pl.ds(r0, SUBLANES), :] + a_ref[pl.ds(r0, SUBLANES), :] * hprev
        u_ref[pl.ds(r0, SUBLANES), :] = hcur
        return hcur[SUBLANES - 1:SUBLANES, :]

    lax.fori_loop(0, s // SUBLANES, group, jnp.zeros((1, w), F32), unroll=8)

    def finish(c, carry):
        r0 = pl.multiple_of(c * LRU_CHUNK, LRU_CHUNK)
        rec = u_ref[pl.ds(r0, LRU_CHUNK), :]
        rs = lax.rsqrt(jnp.mean(rec * rec, axis=-1, keepdims=True) + EPS)
        z = zl_ref[0, pl.ds(r0, LRU_CHUNK), :].astype(F32)
        y_ref[0, pl.ds(r0, LRU_CHUNK), :] = ((rec * rs) * gain_ref[...] * (z * _sigmoid(z))).astype(BF16)
        return carry

    lax.fori_loop(0, nchunk, finish, 0)


def _rglru(xl, zl, conv_w, conv_b, wr_bd, wi_bd, b_r, b_i, lam, gain):
    bsz, s, w = xl.shape
    row = lambda a: a.reshape(1, w)
    seq_spec = pl.BlockSpec((1, s, w), lambda b: (b, 0, 0))
    vec_spec = pl.BlockSpec((1, w), lambda b: (0, 0))
    return pl.pallas_call(
        _lru_kernel,
        grid=(bsz,),
        in_specs=[seq_spec, seq_spec,
                  pl.BlockSpec((CONV_WIDTH, w), lambda b: (0, 0)), vec_spec,
                  pl.BlockSpec((w, w), lambda b: (0, 0)), pl.BlockSpec((w, w), lambda b: (0, 0)),
                  vec_spec, vec_spec, vec_spec, vec_spec],
        out_specs=seq_spec,
        out_shape=jax.ShapeDtypeStruct((bsz, s, w), BF16),
        scratch_shapes=[pltpu.VMEM((s + SUBLANES, w), F32),
                        pltpu.VMEM((s, w), F32), pltpu.VMEM((s, w), F32)],
        compiler_params=pltpu.CompilerParams(
            dimension_semantics=("arbitrary",), vmem_limit_bytes=VMEM_LIMIT),
        name="rglru",
    )(xl, zl, conv_w, row(conv_b), wr_bd, wi_bd, row(b_r), row(b_i), row(lam), row(gain))


def _outproj_kernel(at_ref, zat_ref, yl_ref, x_ref, mod_ref, ga_ref, wo_ref, fg_ref, o_ref,
                    *, final_norm):
    d = x_ref.shape[2]
    w = ATTN_WIDTH
    at = at_ref[0].astype(F32)
    rs = lax.rsqrt(jnp.mean(at * at, axis=0, keepdims=True) + EPS)
    z = zat_ref[0].astype(F32)
    ya = ((at * rs) * ga_ref[...] * (z * _sigmoid(z))).astype(BF16)
    tn = (((0,), (0,)), ((), ()))
    y = lax.dot_general(ya, wo_ref[0:w, :], tn, preferred_element_type=F32)
    y = y + jnp.dot(yl_ref[0], wo_ref[w:2 * w, :], preferred_element_type=F32)
    gate = mod_ref[0, :, 2 * d:3 * d]
    xo = x_ref[0] + gate * y
    if final_norm:
        xo = (xo * lax.rsqrt(jnp.mean(xo * xo, axis=-1, keepdims=True) + EPS)) * fg_ref[...]
    o_ref[0] = xo


def _outproj(at, zat, yl, x, mod3, gain_a, w_out, final_gain, ts, final_norm):
    bsz, s, d = x.shape
    w = ATTN_WIDTH
    return pl.pallas_call(
        functools.partial(_outproj_kernel, final_norm=final_norm),
        grid=(bsz, s // ts),
        in_specs=[
            pl.BlockSpec((1, w, ts), lambda b, i: (b, 0, i)),
            pl.BlockSpec((1, w, ts), lambda b, i: (b, 0, i)),
            pl.BlockSpec((1, ts, w), lambda b, i: (b, i, 0)),
            pl.BlockSpec((1, ts, d), lambda b, i: (b, i, 0)),
            pl.BlockSpec((1, 1, 3 * d), lambda b, i: (b, 0, 0)),
            pl.BlockSpec((w, 1), lambda b, i: (0, 0)),
            pl.BlockSpec((2 * w, d), lambda b, i: (0, 0)),
            pl.BlockSpec((1, d), lambda b, i: (0, 0)),
        ],
        out_specs=pl.BlockSpec((1, ts, d), lambda b, i: (b, i, 0)),
        out_shape=jax.ShapeDtypeStruct((bsz, s, d), F32),
        compiler_params=pltpu.CompilerParams(
            dimension_semantics=("arbitrary", "arbitrary"), vmem_limit_bytes=VMEM_LIMIT),
        name="outproj",
    )(at, zat, yl, x, mod3, gain_a, w_out, final_gain)


def _rope_tables(s):
    pos = jnp.arange(s, dtype=F32)
    inv_freq = ROPE_THETA ** (-jnp.arange(0, HEAD_DIM, 2, dtype=F32) / HEAD_DIM)
    ang = pos[:, None] * inv_freq[None, :]
    cos, sin = jnp.cos(ang), jnp.sin(ang)
    cosn = jnp.tile(cos, (1, LANES // HALF_DIM))
    sinn = jnp.tile(jnp.concatenate([-sin, sin], axis=1), (1, LANES // HEAD_DIM))
    cost = jnp.concatenate([cos.T, cos.T], axis=0)
    sint = jnp.concatenate([-sin.T, sin.T], axis=0)
    return cosn, sinn, cost, sint


def _block_diag(wg):
    g, n, _ = wg.shape
    eye = jnp.eye(g, dtype=wg.dtype)
    return jnp.einsum("gij,gh->gihj", wg, eye).reshape(g * n, g * n)


def kernel(x, c, w_mod, b_mod, norm_gain, w_in, conv_w, conv_b, w_rgate, b_rgate,
           w_igate, b_igate, lru_lambda, attn_out_gain, lru_out_gain, w_out, final_gain):
    bsz, s, d = x.shape
    depth = w_in.shape[0]
    w = ATTN_WIDTH
    assert s % 512 == 0 and d % LANES == 0 and w_in.shape[2] == 6 * w
    ts = 512
    cosn, sinn, cost, sint = _rope_tables(s)
    for l in range(depth):
        mod3 = _modulation(c, w_mod[l], b_mod[l]).reshape(bsz, 1, 3 * d)
        wl = w_in[l]
        wq, wk, wv, wza, wxl, wzl = [wl[:, i * w:(i + 1) * w] for i in range(6)]
        w_nat = jnp.concatenate([wk, wxl, wzl], axis=1).astype(BF16)
        w_t = jnp.concatenate([wq, wv, wza], axis=1).T.astype(BF16)
        qt, vt, zat, k, xl, zl = _inproj(x, mod3, norm_gain[l].reshape(1, d), w_nat, w_t,
                                         cosn, sinn, cost, sint, ts)
        at = _attention(qt, k, vt)
        yl = _rglru(xl, zl, conv_w[l], conv_b[l],
                    _block_diag(w_rgate[l]).astype(BF16), _block_diag(w_igate[l]).astype(BF16),
                    b_rgate[l].reshape(-1), b_igate[l].reshape(-1), lru_lambda[l], lru_out_gain[l])
        x = _outproj(at, zat, yl, x, mod3, attn_out_gain[l].reshape(w, 1),
                     w_out[l].astype(BF16), final_gain.reshape(1, d), ts,
                     final_norm=(l == depth - 1))
    return x
```

```python
import functools

import jax
import jax.numpy as jnp
from jax import lax
from jax.experimental import pallas as pl
from jax.experimental.pallas import tpu as pltpu

ATTN_HEADS = 8
HEAD_DIM = 64
HALF_DIM = HEAD_DIM // 2
ATTN_WIDTH = ATTN_HEADS * HEAD_DIM
LRU_BLOCKS = 8
MOBA_BLOCK = 256
MOBA_TOPK = 3
CONV_WIDTH = 4
LRU_C = 8.0
ROPE_THETA = 10000.0
EPS = 1e-6

LANES = 128
SUBLANES = 8
BF16_ROWS = 16
MASK_VALUE = -1e30
LOG2_E = 1.4426950408889634
VMEM_LIMIT = 56 * 1024 * 1024

F32 = jnp.float32
BF16 = jnp.bfloat16


def _sigmoid(z):
    return 1.0 / (1.0 + jnp.exp(-z))


def _split3(a):
    hi = a.astype(BF16)
    r1 = a - hi.astype(F32)
    mid = r1.astype(BF16)
    lo = (r1 - mid.astype(F32)).astype(BF16)
    return hi, mid, lo


def _mod_kernel(c_ref, w_ref, b_ref, o_ref):
    c = c_ref[...]
    act = c * _sigmoid(c)
    a_hi, a_mid, _ = _split3(act)
    w_hi, w_mid, _ = _split3(w_ref[...])
    dot = functools.partial(jnp.dot, preferred_element_type=F32)
    acc = dot(a_hi, w_hi) + dot(a_hi, w_mid) + dot(a_mid, w_hi)
    o_ref[...] = acc + b_ref[...]


def _modulation(c, w_mod, b_mod):
    bsz, d = c.shape
    n = w_mod.shape[1]
    tn = 512
    return pl.pallas_call(
        _mod_kernel,
        grid=(n // tn,),
        in_specs=[
            pl.BlockSpec((bsz, d), lambda j: (0, 0)),
            pl.BlockSpec((d, tn), lambda j: (0, j)),
            pl.BlockSpec((1, tn), lambda j: (0, j)),
        ],
        out_specs=pl.BlockSpec((bsz, tn), lambda j: (0, j)),
        out_shape=jax.ShapeDtypeStruct((bsz, n), F32),
        compiler_params=pltpu.CompilerParams(
            dimension_semantics=("arbitrary",), vmem_limit_bytes=VMEM_LIMIT),
        name="modulation",
    )(c, w_mod, b_mod.reshape(1, n))


LRU_CHUNK = 128
PROJ_PIECE = 256
LRU_QUAD = 256


def _inproj_lru_kernel(x_ref, mod_ref, gain_ref, wn_ref, wt_ref, cosn_ref, sinn_ref,
                       cost_ref, sint_ref, cw_ref, cb_ref, wr_ref, wi_ref, br_ref, bi_ref,
                       lam_ref, lg_ref,
                       qt_ref, vt_ref, zat_ref, k_ref, yl_ref,
                       ext_buf, zl_buf, a_buf, u_buf, hc_ref):
    d = x_ref.shape[2]
    ts = x_ref.shape[1]
    w = ATTN_WIDTH
    pad = SUBLANES
    z0 = jnp.minimum(pl.program_id(1), 0)
    ext_ref, zl_ref, a_ref, u_ref = ext_buf.at[z0], zl_buf.at[z0], a_buf.at[z0], u_buf.at[z0]

    @pl.when(pl.program_id(1) == 0)
    def _():
        ext_ref[0:pad, :] = jnp.zeros((pad, w), F32)
        hc_ref[...] = jnp.zeros(hc_ref.shape, F32)

    x = x_ref[0]
    shift = mod_ref[0, :, 0:d]
    scale = mod_ref[0, :, d:2 * d]
    rs = lax.rsqrt(jnp.mean(x * x, axis=-1, keepdims=True) + EPS)
    h = (x * rs) * (gain_ref[...] * (1.0 + scale)) + shift
    hb = h.astype(BF16)

    ext_ref[pad:pad + ts, :] = jnp.dot(hb, wn_ref[:, w:2 * w], preferred_element_type=F32)
    zl_ref[...] = jnp.dot(hb, wn_ref[:, 2 * w:3 * w], preferred_element_type=F32)

    nt = (((1,), (1,)), ((), ()))
    half = PROJ_PIECE
    n_piece = w // PROJ_PIECE

    def k_piece(p):
        kk = jnp.dot(hb, wn_ref[:, p * half:(p + 1) * half], preferred_element_type=F32)
        cosn = cosn_ref[...]
        sinn = sinn_ref[...]
        lane = lax.broadcasted_iota(jnp.int32, cosn.shape, 1)
        first_half = (lane % HEAD_DIM) < HALF_DIM
        for g in range(half // LANES):
            kg = kk[:, g * LANES:(g + 1) * LANES]
            swapped = jnp.where(first_half,
                                pltpu.roll(kg, LANES - HALF_DIM, axis=1),
                                pltpu.roll(kg, HALF_DIM, axis=1))
            lo = p * half + g * LANES
            k_ref[0, :, lo:lo + LANES] = (kg * cosn + swapped * sinn).astype(BF16)

    def q_piece(p):
        qt = lax.dot_general(wt_ref[p * half:(p + 1) * half, :], hb, nt,
                             preferred_element_type=F32)
        cost = cost_ref[...]
        sint = sint_ref[...]
        qscale = HEAD_DIM ** -0.5 * LOG2_E
        for hd in range(half // HEAD_DIM):
            t = qt[hd * HEAD_DIM:(hd + 1) * HEAD_DIM]
            swapped = jnp.concatenate([t[HALF_DIM:], t[:HALF_DIM]], axis=0)
            lo = p * half + hd * HEAD_DIM
            qt_ref[0, lo:lo + HEAD_DIM, :] = ((t * cost + swapped * sint) * qscale).astype(BF16)

    def fm_piece(out_ref, base, p):
        lo = p * half
        out_ref[0, lo:lo + half, :] = lax.dot_general(
            wt_ref[base + lo:base + lo + half, :], hb, nt,
            preferred_element_type=F32).astype(BF16)

    lam = lam_ref[...]
    neg_lam = -lam
    softplus = jnp.maximum(neg_lam, 0.0) + jnp.log1p(jnp.exp(-jnp.abs(neg_lam)))
    decay = (-LRU_C * LOG2_E) * softplus
    state = {"h": hc_ref[...]}

    def quad_dot(xb, w_ref):
        return jnp.concatenate(
            [jnp.dot(xb[:, q:q + LRU_QUAD], w_ref[q:q + LRU_QUAD, q:q + LRU_QUAD],
                     preferred_element_type=F32) for q in range(0, w, LRU_QUAD)], axis=1)

    def lru_gates(c):
        base = pad + c * LRU_CHUNK
        xc = cb_ref[...] + cw_ref[CONV_WIDTH - 1:CONV_WIDTH, :] * ext_ref[base:base + LRU_CHUNK, :]
        for tap in range(1, CONV_WIDTH):
            xc = xc + (cw_ref[CONV_WIDTH - 1 - tap:CONV_WIDTH - tap, :] *
                       ext_ref[base - tap:base - tap + LRU_CHUNK, :])
        xcb = xc.astype(BF16)
        r = _sigmoid(quad_dot(xcb, wr_ref) + br_ref[...])
        ig = _sigmoid(quad_dot(xcb, wi_ref) + bi_ref[...])
        a = jnp.exp2(decay * r)
        y = 1.0 - a * a
        u = jnp.where(y > 0.0, y * lax.rsqrt(y), 0.0) * ig * xc
        grp = (LRU_CHUNK // SUBLANES, SUBLANES, w)
        a = a.reshape(grp)
        u = u.reshape(grp)
        rowg = lax.broadcasted_iota(jnp.int32, grp, 1)
        for sh in (1, 2, 4):
            ok = rowg >= sh
            a_prev = pltpu.roll(a, sh, axis=1)
            u_prev = pltpu.roll(u, sh, axis=1)
            u = jnp.where(ok, a * u_prev + u, u)
            a = jnp.where(ok, a * a_prev, a)
        a_ref[c * LRU_CHUNK:(c + 1) * LRU_CHUNK, :] = a.reshape(LRU_CHUNK, w)
        u_ref[c * LRU_CHUNK:(c + 1) * LRU_CHUNK, :] = u.reshape(LRU_CHUNK, w)

    def lru_carry(c):
        hprev = state["h"]
        for g in range(LRU_CHUNK // SUBLANES):
            r0 = c * LRU_CHUNK + g * SUBLANES
            hcur = u_ref[r0:r0 + SUBLANES, :] + a_ref[r0:r0 + SUBLANES, :] * hprev
            u_ref[r0:r0 + SUBLANES, :] = hcur
            hprev = hcur[SUBLANES - 1:SUBLANES, :]
        state["h"] = hprev

    def lru_finish(c):
        rows = slice(c * LRU_CHUNK, (c + 1) * LRU_CHUNK)
        rec = u_ref[rows, :]
        rsq = lax.rsqrt(jnp.mean(rec * rec, axis=-1, keepdims=True) + EPS)
        z = zl_ref[rows, :]
        yl_ref[0, rows, :] = ((rec * rsq) * lg_ref[...] * (z * _sigmoid(z))).astype(BF16)

    mxu_pieces = ([functools.partial(k_piece, p) for p in range(n_piece)] +
                  [functools.partial(q_piece, p) for p in range(n_piece)] +
                  [functools.partial(fm_piece, vt_ref, w, p) for p in range(n_piece)] +
                  [functools.partial(fm_piece, zat_ref, 2 * w, p) for p in range(n_piece)])
    lru_pieces = []
    for c in range(ts // LRU_CHUNK):
        lru_pieces += [functools.partial(lru_gates, c), functools.partial(lru_carry, c),
                       functools.partial(lru_finish, c)]
    done = 0
    for i, piece in enumerate(mxu_pieces):
        piece()
        upto = (i + 1) * len(lru_pieces) // len(mxu_pieces)
        for lp in lru_pieces[done:upto]:
            lp()
        done = upto

    hc_ref[...] = state["h"]
    ext_ref[0:pad, :] = ext_ref[ts:ts + pad, :]


def _inproj_lru(x, mod3, gain, w_nat, w_t, cosn, sinn, cost, sint,
                conv_w, conv_b, wr_bd, wi_bd, b_r, b_i, lam, lru_gain, ts):
    bsz, s, d = x.shape
    w = ATTN_WIDTH
    fm = jax.ShapeDtypeStruct((bsz, w, s), BF16)
    sm = jax.ShapeDtypeStruct((bsz, s, w), BF16)
    fm_spec = pl.BlockSpec((1, w, ts), lambda b, i: (b, 0, i))
    sm_spec = pl.BlockSpec((1, ts, w), lambda b, i: (b, i, 0))
    vec_spec = pl.BlockSpec((1, w), lambda b, i: (0, 0))
    mat_spec = pl.BlockSpec((w, w), lambda b, i: (0, 0))
    row = lambda a: a.reshape(1, w)
    return pl.pallas_call(
        _inproj_lru_kernel,
        grid=(bsz, s // ts),
        in_specs=[
            pl.BlockSpec((1, ts, d), lambda b, i: (b, i, 0)),
            pl.BlockSpec((1, 1, 3 * d), lambda b, i: (b, 0, 0)),
            pl.BlockSpec((1, d), lambda b, i: (0, 0)),
            pl.BlockSpec((d, 3 * w), lambda b, i: (0, 0)),
            pl.BlockSpec((3 * w, d), lambda b, i: (0, 0)),
            pl.BlockSpec((ts, LANES), lambda b, i: (i, 0)),
            pl.BlockSpec((ts, LANES), lambda b, i: (i, 0)),
            pl.BlockSpec((HEAD_DIM, ts), lambda b, i: (0, i)),
            pl.BlockSpec((HEAD_DIM, ts), lambda b, i: (0, i)),
            pl.BlockSpec((CONV_WIDTH, w), lambda b, i: (0, 0)), vec_spec,
            mat_spec, mat_spec, vec_spec, vec_spec, vec_spec, vec_spec,
        ],
        out_specs=[fm_spec, fm_spec, fm_spec, sm_spec, sm_spec],
        out_shape=[fm, fm, fm, sm, sm],
        scratch_shapes=[pltpu.VMEM((1, ts + SUBLANES, w), F32),
                        pltpu.VMEM((1, ts, w), F32),
                        pltpu.VMEM((1, ts, w), F32),
                        pltpu.VMEM((1, ts, w), F32),
                        pltpu.VMEM((1, w), F32)],
        compiler_params=pltpu.CompilerParams(
            dimension_semantics=("arbitrary", "arbitrary"), vmem_limit_bytes=VMEM_LIMIT),
        name="inproj_rglru",
    )(x, mod3, gain, w_nat, w_t, cosn, sinn, cost, sint,
      conv_w, row(conv_b), wr_bd, wi_bd, row(b_r), row(b_i), row(lam), row(lru_gain))


HEADS_PER_STEP = LANES // HEAD_DIM
BLOCK_SHIFT = MOBA_BLOCK.bit_length() - 1
assert 1 << BLOCK_SHIFT == MOBA_BLOCK
ATTN_LEAD = 2
ATTN_RING = 12


def _attn_kernel(qt_ref, k_ref, vt_ref, oh_ref, avg_ref, o_ref, ka_ref, qa_ref, va_ref, s_ref):
    s = qt_ref.shape[2]
    nblk = s // MOBA_BLOCK
    kp = k_ref[0]

    kmean = jnp.dot(avg_ref[...], kp, preferred_element_type=F32)
    km3 = jnp.concatenate(_split3(kmean), axis=0)

    jrow = lax.broadcasted_iota(jnp.int32, (nblk, s), 0)
    qblk = lax.shift_right_logical(lax.broadcasted_iota(jnp.int32, (nblk, s), 1), BLOCK_SHIFT)
    cand = jrow < qblk
    qb = qblk[0:1, :]
    kr = lax.broadcasted_iota(jnp.int32, (MOBA_BLOCK, MOBA_BLOCK), 0)
    qc = lax.broadcasted_iota(jnp.int32, (MOBA_BLOCK, MOBA_BLOCK), 1)
    causal = kr <= qc

    for hh in range(HEADS_PER_STEP):
        head_lo = hh * HEAD_DIM
        aux_lo = HEAD_DIM - head_lo
        rows_h = slice(head_lo, head_lo + HEAD_DIM)

        ka_ref[hh] = oh_ref[hh]
        ka_ref[hh, :, rows_h] = k_ref[0, :, rows_h]

        va_ref[hh, 0:HEAD_DIM, :] = vt_ref[0, rows_h, :]
        va_ref[hh, HEAD_DIM:, :] = jnp.ones((BF16_ROWS, s), BF16)

        qa_ref[hh] = jnp.zeros(qa_ref.shape[1:], BF16)
        qa_ref[hh, rows_h, :] = qt_ref[0, rows_h, :]
        g3 = jnp.dot(km3, qa_ref[hh], preferred_element_type=F32)
        gate = (g3[0:BF16_ROWS] + g3[BF16_ROWS:2 * BF16_ROWS]) + g3[2 * BF16_ROWS:]
        gate = gate[0:nblk]

        bias_rows = []
        for j in range(nblk):
            gj = gate[j:j + 1, :]
            beats = cand & ((gate > gj) | ((gate == gj) & (jrow < j)))
            rank = jnp.sum(beats.astype(F32), axis=0, keepdims=True)
            keep = ((qb > j) & (rank < MOBA_TOPK)) | (qb == j)
            bias_rows.append(jnp.where(keep, 0.0, MASK_VALUE))
        bias = jnp.concatenate(bias_rows + [jnp.zeros((BF16_ROWS - nblk, s), F32)], axis=0)
        qa_ref[hh, aux_lo:aux_lo + BF16_ROWS, :] = bias.astype(BF16)

    items = ([(i, 0) for i in range(nblk)] +
             [(i, hh) for hh in range(1, HEADS_PER_STEP) for i in reversed(range(nblk))])
    piece_list = [(n, j) for n, (i, _) in enumerate(items) for j in range(i + 1)]
    item_end = {}
    for t, (n, _) in enumerate(piece_list):
        item_end[n] = t + 1
    state = [dict() for _ in items]
    ring = s_ref.shape[0]
    slot0 = jnp.minimum(pl.program_id(1), 0)

    def blk_rows(j):
        return slice(j * MOBA_BLOCK, (j + 1) * MOBA_BLOCK)

    def score_piece(t):
        n, j = piece_list[t]
        i, hh = items[n]
        sc = jnp.dot(ka_ref[hh, blk_rows(j), :], qa_ref[hh, :, blk_rows(i)],
                     preferred_element_type=F32)
        if j == i:
            sc = jnp.where(causal, sc, MASK_VALUE)
        s_ref[slot0 + t % ring] = sc
        cm = jnp.max(sc, axis=0, keepdims=True)
        st = state[n]
        st["m"] = cm if j == 0 else jnp.maximum(st["m"], cm)

    def softmax_piece(t):
        n, j = piece_list[t]
        i, hh = items[n]
        rows_h = slice(hh * HEAD_DIM, (hh + 1) * HEAD_DIM)
        st = state[n]
        p = jnp.exp2(s_ref[slot0 + t % ring] - st["m"])
        pv = jnp.dot(va_ref[hh, :, blk_rows(j)], p.astype(BF16),
                     preferred_element_type=F32)
        st["acc"] = pv if j == 0 else st["acc"] + pv
        if j == i:
            acc = st["acc"]
            o_ref[0, rows_h, blk_rows(i)] = (
                acc[0:HEAD_DIM] / acc[HEAD_DIM:HEAD_DIM + 1]).astype(BF16)

    total = len(piece_list)
    scored = softmaxed = 0
    while softmaxed < total:
        if scored < total:
            assert scored - softmaxed < ring, "score ring too small for the softmax lag"
            score_piece(scored)
            scored += 1
        ready = min(total, item_end[piece_list[softmaxed][0]] + ATTN_LEAD)
        if scored >= ready:
            softmax_piece(softmaxed)
            softmaxed += 1


def _attention(qt, k, vt):
    bsz, w, s = qt.shape
    blk = jnp.arange(s, dtype=jnp.int32) // MOBA_BLOCK
    lane = jnp.arange(LANES, dtype=jnp.int32)
    oh = jnp.stack([(lane[None, :] == (HEAD_DIM - hh * HEAD_DIM) + blk[:, None])
                    for hh in range(HEADS_PER_STEP)]).astype(BF16)
    avg = jnp.where(jnp.arange(BF16_ROWS, dtype=jnp.int32)[:, None] == blk[None, :],
                    1.0 / MOBA_BLOCK, 0.0).astype(BF16)
    pair_fm = pl.BlockSpec((1, LANES, s), lambda b, p: (b, p, 0))
    return pl.pallas_call(
        _attn_kernel,
        grid=(bsz, ATTN_HEADS // HEADS_PER_STEP),
        in_specs=[
            pair_fm,
            pl.BlockSpec((1, s, LANES), lambda b, p: (b, 0, p)),
            pair_fm,
            pl.BlockSpec((HEADS_PER_STEP, s, LANES), lambda b, p: (0, 0, 0)),
            pl.BlockSpec((BF16_ROWS, s), lambda b, p: (0, 0)),
        ],
        out_specs=pair_fm,
        out_shape=jax.ShapeDtypeStruct((bsz, w, s), BF16),
        scratch_shapes=[pltpu.VMEM((HEADS_PER_STEP, s, LANES), BF16),
                        pltpu.VMEM((HEADS_PER_STEP, LANES, s), BF16),
                        pltpu.VMEM((HEADS_PER_STEP, HEAD_DIM + BF16_ROWS, s), BF16),
                        pltpu.VMEM((ATTN_RING, MOBA_BLOCK, MOBA_BLOCK), F32)],
        compiler_params=pltpu.CompilerParams(
            dimension_semantics=("arbitrary", "arbitrary"), vmem_limit_bytes=VMEM_LIMIT),
        name="moba_attention",
    )(qt, k, vt, oh, avg)


def _outproj_kernel(at_ref, zat_ref, yl_ref, x_ref, mod_ref, ga_ref, wo_ref, fg_ref, o_ref,
                    *, final_norm):
    d = x_ref.shape[2]
    w = ATTN_WIDTH
    at = at_ref[0].astype(F32)
    rs = lax.rsqrt(jnp.mean(at * at, axis=0, keepdims=True) + EPS)
    z = zat_ref[0].astype(F32)
    ya = ((at * rs) * ga_ref[...] * (z * _sigmoid(z))).astype(BF16)
    tn = (((0,), (0,)), ((), ()))
    y = lax.dot_general(ya, wo_ref[0:w, :], tn, preferred_element_type=F32)
    y = y + jnp.dot(yl_ref[0], wo_ref[w:2 * w, :], preferred_element_type=F32)
    gate = mod_ref[0, :, 2 * d:3 * d]
    xo = x_ref[0] + gate * y
    if final_norm:
        xo = (xo * lax.rsqrt(jnp.mean(xo * xo, axis=-1, keepdims=True) + EPS)) * fg_ref[...]
    o_ref[0] = xo


def _outproj(at, zat, yl, x, mod3, gain_a, w_out, final_gain, ts, final_norm):
    bsz, s, d = x.shape
    w = ATTN_WIDTH
    return pl.pallas_call(
        functools.partial(_outproj_kernel, final_norm=final_norm),
        grid=(bsz, s // ts),
        in_specs=[
            pl.BlockSpec((1, w, ts), lambda b, i: (b, 0, i)),
            pl.BlockSpec((1, w, ts), lambda b, i: (b, 0, i)),
            pl.BlockSpec((1, ts, w), lambda b, i: (b, i, 0)),
            pl.BlockSpec((1, ts, d), lambda b, i: (b, i, 0)),
            pl.BlockSpec((1, 1, 3 * d), lambda b, i: (b, 0, 0)),
            pl.BlockSpec((w, 1), lambda b, i: (0, 0)),
            pl.BlockSpec((2 * w, d), lambda b, i: (0, 0)),
            pl.BlockSpec((1, d), lambda b, i: (0, 0)),
        ],
        out_specs=pl.BlockSpec((1, ts, d), lambda b, i: (b, i, 0)),
        out_shape=jax.ShapeDtypeStruct((bsz, s, d), F32),
        compiler_params=pltpu.CompilerParams(
            dimension_semantics=("arbitrary", "arbitrary"), vmem_limit_bytes=VMEM_LIMIT),
        name="outproj",
    )(at, zat, yl, x, mod3, gain_a, w_out, final_gain)


def _rope_tables(s):
    pos = jnp.arange(s, dtype=F32)
    inv_freq = ROPE_THETA ** (-jnp.arange(0, HEAD_DIM, 2, dtype=F32) / HEAD_DIM)
    ang = pos[:, None] * inv_freq[None, :]
    cos, sin = jnp.cos(ang), jnp.sin(ang)
    cosn = jnp.tile(cos, (1, LANES // HALF_DIM))
    sinn = jnp.tile(jnp.concatenate([-sin, sin], axis=1), (1, LANES // HEAD_DIM))
    cost = jnp.concatenate([cos.T, cos.T], axis=0)
    sint = jnp.concatenate([-sin.T, sin.T], axis=0)
    return cosn, sinn, cost, sint


def _block_diag(wg):
    g, n, _ = wg.shape
    eye = jnp.eye(g, dtype=wg.dtype)
    return jnp.einsum("gij,gh->gihj", wg, eye).reshape(g * n, g * n)


def kernel(x, c, w_mod, b_mod, norm_gain, w_in, conv_w, conv_b, w_rgate, b_rgate,
           w_igate, b_igate, lru_lambda, attn_out_gain, lru_out_gain, w_out, final_gain):
    bsz, s, d = x.shape
    depth = w_in.shape[0]
    w = ATTN_WIDTH
    assert s % 512 == 0 and d % LANES == 0 and w_in.shape[2] == 6 * w
    ts = 512
    cosn, sinn, cost, sint = _rope_tables(s)
    for l in range(depth):
        mod3 = _modulation(c, w_mod[l], b_mod[l]).reshape(bsz, 1, 3 * d)
        wl = w_in[l]
        wq, wk, wv, wza, wxl, wzl = [wl[:, i * w:(i + 1) * w] for i in range(6)]
        w_nat = jnp.concatenate([wk, wxl, wzl], axis=1).astype(BF16)
        w_t = jnp.concatenate([wq, wv, wza], axis=1).T.astype(BF16)
        qt, vt, zat, k, yl = _inproj_lru(
            x, mod3, norm_gain[l].reshape(1, d), w_nat, w_t, cosn, sinn, cost, sint,
            conv_w[l], conv_b[l],
            _block_diag(w_rgate[l]).astype(BF16), _block_diag(w_igate[l]).astype(BF16),
            b_rgate[l].reshape(-1), b_igate[l].reshape(-1), lru_lambda[l], lru_out_gain[l], ts)
        at = _attention(qt, k, vt)
        x = _outproj(at, zat, yl, x, mod3, attn_out_gain[l].reshape(w, 1),
                     w_out[l].astype(BF16), final_gain.reshape(1, d), ts,
                     final_norm=(l == depth - 1))
    return x
```

```python
import functools

import jax
import jax.numpy as jnp
from jax import lax
from jax.experimental import pallas as pl
from jax.experimental.pallas import tpu as pltpu

ATTN_HEADS = 8
HEAD_DIM = 64
HALF_DIM = HEAD_DIM // 2
ATTN_WIDTH = ATTN_HEADS * HEAD_DIM
LRU_BLOCKS = 8
MOBA_BLOCK = 256
MOBA_TOPK = 3
CONV_WIDTH = 4
LRU_C = 8.0
ROPE_THETA = 10000.0
EPS = 1e-6

LANES = 128
SUBLANES = 8
BF16_ROWS = 16
MASK_VALUE = -1e30
LOG2_E = 1.4426950408889634
VMEM_LIMIT = 56 * 1024 * 1024

F32 = jnp.float32
BF16 = jnp.bfloat16


def _sigmoid(z):
    return 1.0 / (1.0 + jnp.exp(-z))


def _split3(a):
    hi = a.astype(BF16)
    r1 = a - hi.astype(F32)
    mid = r1.astype(BF16)
    lo = (r1 - mid.astype(F32)).astype(BF16)
    return hi, mid, lo


def _mod_kernel(c_ref, w_ref, b_ref, o_ref):
    c = c_ref[...]
    act = c * _sigmoid(c)
    a_hi, a_mid, _ = _split3(act)
    w_hi, w_mid, _ = _split3(w_ref[...])
    dot = functools.partial(jnp.dot, preferred_element_type=F32)
    acc = dot(a_hi, w_hi) + dot(a_hi, w_mid) + dot(a_mid, w_hi)
    o_ref[...] = acc + b_ref[...]


def _modulation(c, w_mod, b_mod):
    bsz, d = c.shape
    n = w_mod.shape[1]
    tn = 512
    return pl.pallas_call(
        _mod_kernel,
        grid=(n // tn,),
        in_specs=[
            pl.BlockSpec((bsz, d), lambda j: (0, 0)),
            pl.BlockSpec((d, tn), lambda j: (0, j)),
            pl.BlockSpec((1, tn), lambda j: (0, j)),
        ],
        out_specs=pl.BlockSpec((bsz, tn), lambda j: (0, j)),
        out_shape=jax.ShapeDtypeStruct((bsz, n), F32),
        compiler_params=pltpu.CompilerParams(
            dimension_semantics=("arbitrary",), vmem_limit_bytes=VMEM_LIMIT),
        name="modulation",
    )(c, w_mod, b_mod.reshape(1, n))


IN_TILE = 512
OUT_TILE = 1024
OUT_SUB = 256
K_COL = 1 * ATTN_WIDTH
XL_COL = 4 * ATTN_WIDTH
ZL_COL = 5 * ATTN_WIDTH
LRU_CHUNK = 128
PROJ_PIECE = 256
LRU_QUAD = 256


def _inproj_lru_kernel(x_ref, mod_ref, gain_ref, wn_ref, wt_ref, cosn_ref, sinn_ref,
                       cost_ref, sint_ref, cw_ref, cb_ref, wr_ref, wi_ref, br_ref, bi_ref,
                       lam_ref, lg_ref,
                       qt_ref, vt_ref, zat_ref, k_ref, yl_ref,
                       ext_buf, zl_buf, a_buf, u_buf, hc_ref):
    d = x_ref.shape[2]
    ts = x_ref.shape[1]
    w = ATTN_WIDTH
    pad = SUBLANES
    z0 = jnp.minimum(pl.program_id(1), 0)
    ext_ref, zl_ref, a_ref, u_ref = ext_buf.at[z0], zl_buf.at[z0], a_buf.at[z0], u_buf.at[z0]

    @pl.when(pl.program_id(1) == 0)
    def _():
        ext_ref[0:pad, :] = jnp.zeros((pad, w), F32)
        hc_ref[...] = jnp.zeros(hc_ref.shape, F32)

    x = x_ref[0]
    shift = mod_ref[0, :, 0:d]
    scale = mod_ref[0, :, d:2 * d]
    rs = lax.rsqrt(jnp.mean(x * x, axis=-1, keepdims=True) + EPS)
    h = (x * rs) * (gain_ref[...] * (1.0 + scale)) + shift
    hb = h.astype(BF16)

    ext_ref[pad:pad + ts, :] = jnp.dot(hb, wn_ref[:, XL_COL:XL_COL + w],
                                       preferred_element_type=F32)
    zl_ref[...] = jnp.dot(hb, wn_ref[:, ZL_COL:ZL_COL + w], preferred_element_type=F32)

    nt = (((1,), (1,)), ((), ()))
    half = PROJ_PIECE
    n_piece = w // PROJ_PIECE

    def k_piece(p):
        kk = jnp.dot(hb, wn_ref[:, K_COL + p * half:K_COL + (p + 1) * half],
                     preferred_element_type=F32)
        cosn = cosn_ref[...]
        sinn = sinn_ref[...]
        lane = lax.broadcasted_iota(jnp.int32, cosn.shape, 1)
        first_half = (lane % HEAD_DIM) < HALF_DIM
        for g in range(half // LANES):
            kg = kk[:, g * LANES:(g + 1) * LANES]
            swapped = jnp.where(first_half,
                                pltpu.roll(kg, LANES - HALF_DIM, axis=1),
                                pltpu.roll(kg, HALF_DIM, axis=1))
            lo = p * half + g * LANES
            k_ref[0, :, lo:lo + LANES] = (kg * cosn + swapped * sinn).astype(BF16)

    def q_piece(p):
        qt = lax.dot_general(wt_ref[p * half:(p + 1) * half, :], hb, nt,
                             preferred_element_type=F32)
        cost = cost_ref[...]
        sint = sint_ref[...]
        qscale = HEAD_DIM ** -0.5 * LOG2_E
        for hd in range(half // HEAD_DIM):
            t = qt[hd * HEAD_DIM:(hd + 1) * HEAD_DIM]
            swapped = jnp.concatenate([t[HALF_DIM:], t[:HALF_DIM]], axis=0)
            lo = p * half + hd * HEAD_DIM
            qt_ref[0, lo:lo + HEAD_DIM, :] = ((t * cost + swapped * sint) * qscale).astype(BF16)

    def fm_piece(out_ref, base, p):
        lo = p * half
        out_ref[0, lo:lo + half, :] = lax.dot_general(
            wt_ref[base + lo:base + lo + half, :], hb, nt,
            preferred_element_type=F32).astype(BF16)

    lam = lam_ref[...]
    neg_lam = -lam
    softplus = jnp.maximum(neg_lam, 0.0) + jnp.log1p(jnp.exp(-jnp.abs(neg_lam)))
    decay = (-LRU_C * LOG2_E) * softplus
    state = {"h": hc_ref[...]}

    def quad_dot(xb, w_ref):
        return jnp.concatenate(
            [jnp.dot(xb[:, q:q + LRU_QUAD], w_ref[q:q + LRU_QUAD, q:q + LRU_QUAD],
                     preferred_element_type=F32) for q in range(0, w, LRU_QUAD)], axis=1)

    def lru_gates(c):
        base = pad + c * LRU_CHUNK
        xc = cb_ref[...] + cw_ref[CONV_WIDTH - 1:CONV_WIDTH, :] * ext_ref[base:base + LRU_CHUNK, :]
        for tap in range(1, CONV_WIDTH):
            xc = xc + (cw_ref[CONV_WIDTH - 1 - tap:CONV_WIDTH - tap, :] *
                       ext_ref[base - tap:base - tap + LRU_CHUNK, :])
        xcb = xc.astype(BF16)
        r = _sigmoid(quad_dot(xcb, wr_ref) + br_ref[...])
        ig = _sigmoid(quad_dot(xcb, wi_ref) + bi_ref[...])
        a = jnp.exp2(decay * r)
        y = 1.0 - a * a
        u = jnp.where(y > 0.0, y * lax.rsqrt(y), 0.0) * ig * xc
        grp = (LRU_CHUNK // SUBLANES, SUBLANES, w)
        a = a.reshape(grp)
        u = u.reshape(grp)
        rowg = lax.broadcasted_iota(jnp.int32, grp, 1)
        for sh in (1, 2, 4):
            ok = rowg >= sh
            a_prev = pltpu.roll(a, sh, axis=1)
            u_prev = pltpu.roll(u, sh, axis=1)
            u = jnp.where(ok, a * u_prev + u, u)
            a = jnp.where(ok, a * a_prev, a)
        a_ref[c * LRU_CHUNK:(c + 1) * LRU_CHUNK, :] = a.reshape(LRU_CHUNK, w)
        u_ref[c * LRU_CHUNK:(c + 1) * LRU_CHUNK, :] = u.reshape(LRU_CHUNK, w)

    def lru_carry(c):
        hprev = state["h"]
        for g in range(LRU_CHUNK // SUBLANES):
            r0 = c * LRU_CHUNK + g * SUBLANES
            hcur = u_ref[r0:r0 + SUBLANES, :] + a_ref[r0:r0 + SUBLANES, :] * hprev
            u_ref[r0:r0 + SUBLANES, :] = hcur
            hprev = hcur[SUBLANES - 1:SUBLANES, :]
        state["h"] = hprev

    def lru_finish(c):
        rows = slice(c * LRU_CHUNK, (c + 1) * LRU_CHUNK)
        rec = u_ref[rows, :]
        rsq = lax.rsqrt(jnp.mean(rec * rec, axis=-1, keepdims=True) + EPS)
        z = zl_ref[rows, :]
        yl_ref[0, rows, :] = ((rec * rsq) * lg_ref[...] * (z * _sigmoid(z))).astype(BF16)

    mxu_pieces = ([functools.partial(k_piece, p) for p in range(n_piece)] +
                  [functools.partial(q_piece, p) for p in range(n_piece)] +
                  [functools.partial(fm_piece, vt_ref, w, p) for p in range(n_piece)] +
                  [functools.partial(fm_piece, zat_ref, 2 * w, p) for p in range(n_piece)])
    lru_pieces = []
    for c in range(ts // LRU_CHUNK):
        lru_pieces += [functools.partial(lru_gates, c), functools.partial(lru_carry, c),
                       functools.partial(lru_finish, c)]
    done = 0
    for i, piece in enumerate(mxu_pieces):
        piece()
        upto = (i + 1) * len(lru_pieces) // len(mxu_pieces)
        for lp in lru_pieces[done:upto]:
            lp()
        done = upto

    hc_ref[...] = state["h"]
    ext_ref[0:pad, :] = ext_ref[ts:ts + pad, :]


def _inproj_lru(x, mod3, gain, w_nat, w_t, cosn, sinn, cost, sint,
                conv_w, conv_b, wr_bd, wi_bd, b_r, b_i, lam, lru_gain, ts):
    bsz, s, d = x.shape
    w = ATTN_WIDTH
    fm = jax.ShapeDtypeStruct((bsz, w, s), BF16)
    sm = jax.ShapeDtypeStruct((bsz, s, w), BF16)
    fm_spec = pl.BlockSpec((1, w, ts), lambda b, i: (b, 0, i))
    sm_spec = pl.BlockSpec((1, ts, w), lambda b, i: (b, i, 0))
    vec_spec = pl.BlockSpec((1, w), lambda b, i: (0, 0))
    mat_spec = pl.BlockSpec((w, w), lambda b, i: (0, 0))
    row = lambda a: a.reshape(1, w)
    return pl.pallas_call(
        _inproj_lru_kernel,
        grid=(bsz, s // ts),
        in_specs=[
            pl.BlockSpec((1, ts, d), lambda b, i: (b, i, 0)),
            pl.BlockSpec((1, 1, 3 * d), lambda b, i: (b, 0, 0)),
            pl.BlockSpec((1, d), lambda b, i: (0, 0)),
            pl.BlockSpec((d, 6 * w), lambda b, i: (0, 0)),
            pl.BlockSpec((3 * w, d), lambda b, i: (0, 0)),
            pl.BlockSpec((ts, LANES), lambda b, i: (i, 0)),
            pl.BlockSpec((ts, LANES), lambda b, i: (i, 0)),
            pl.BlockSpec((HEAD_DIM, ts), lambda b, i: (0, i)),
            pl.BlockSpec((HEAD_DIM, ts), lambda b, i: (0, i)),
            pl.BlockSpec((CONV_WIDTH, w), lambda b, i: (0, 0)), vec_spec,
            mat_spec, mat_spec, vec_spec, vec_spec, vec_spec, vec_spec,
        ],
        out_specs=[fm_spec, fm_spec, fm_spec, sm_spec, sm_spec],
        out_shape=[fm, fm, fm, sm, sm],
        scratch_shapes=[pltpu.VMEM((1, ts + SUBLANES, w), F32),
                        pltpu.VMEM((1, ts, w), F32),
                        pltpu.VMEM((1, ts, w), F32),
                        pltpu.VMEM((1, ts, w), F32),
                        pltpu.VMEM((1, w), F32)],
        compiler_params=pltpu.CompilerParams(
            dimension_semantics=("arbitrary", "arbitrary"), vmem_limit_bytes=VMEM_LIMIT),
        name="inproj_rglru",
    )(x, mod3, gain, w_nat, w_t, cosn, sinn, cost, sint,
      conv_w, row(conv_b), wr_bd, wi_bd, row(b_r), row(b_i), row(lam), row(lru_gain))


HEADS_PER_STEP = LANES // HEAD_DIM
BLOCK_SHIFT = MOBA_BLOCK.bit_length() - 1
assert 1 << BLOCK_SHIFT == MOBA_BLOCK
ATTN_LEAD = 5
ATTN_RING = 16


def _attn_kernel(qt_ref, k_ref, vt_ref, oh_ref, avg_ref, o_ref, ka_ref, qa_ref, va_ref, s_ref):
    s = qt_ref.shape[2]
    nblk = s // MOBA_BLOCK
    kp = k_ref[0]

    kmean = jnp.dot(avg_ref[...], kp, preferred_element_type=F32)
    km3 = jnp.concatenate(_split3(kmean), axis=0)

    jrow = lax.broadcasted_iota(jnp.int32, (nblk, s), 0)
    qblk = lax.shift_right_logical(lax.broadcasted_iota(jnp.int32, (nblk, s), 1), BLOCK_SHIFT)
    cand = jrow < qblk
    qb = qblk[0:1, :]
    kr = lax.broadcasted_iota(jnp.int32, (MOBA_BLOCK, MOBA_BLOCK), 0)
    qc = lax.broadcasted_iota(jnp.int32, (MOBA_BLOCK, MOBA_BLOCK), 1)
    causal = kr <= qc

    for hh in range(HEADS_PER_STEP):
        head_lo = hh * HEAD_DIM
        aux_lo = HEAD_DIM - head_lo
        rows_h = slice(head_lo, head_lo + HEAD_DIM)

        ka_ref[hh] = oh_ref[hh]
        ka_ref[hh, :, rows_h] = k_ref[0, :, rows_h]

        va_ref[hh, 0:HEAD_DIM, :] = vt_ref[0, rows_h, :]
        va_ref[hh, HEAD_DIM:, :] = jnp.ones((BF16_ROWS, s), BF16)

        qa_ref[hh] = jnp.zeros(qa_ref.shape[1:], BF16)
        qa_ref[hh, rows_h, :] = qt_ref[0, rows_h, :]
        g3 = jnp.dot(km3, qa_ref[hh], preferred_element_type=F32)
        gate = (g3[0:BF16_ROWS] + g3[BF16_ROWS:2 * BF16_ROWS]) + g3[2 * BF16_ROWS:]
        gate = gate[0:nblk]

        bias_rows = []
        for j in range(nblk):
            gj = gate[j:j + 1, :]
            beats = cand & ((gate > gj) | ((gate == gj) & (jrow < j)))
            rank = jnp.sum(beats.astype(F32), axis=0, keepdims=True)
            keep = ((qb > j) & (rank < MOBA_TOPK)) | (qb == j)
            bias_rows.append(jnp.where(keep, 0.0, MASK_VALUE))
        bias = jnp.concatenate(bias_rows + [jnp.zeros((BF16_ROWS - nblk, s), F32)], axis=0)
        qa_ref[hh, aux_lo:aux_lo + BF16_ROWS, :] = bias.astype(BF16)

    items = ([(i, 0) for i in range(nblk)] +
             [(i, hh) for hh in range(1, HEADS_PER_STEP) for i in reversed(range(nblk))])
    piece_list = [(n, j) for n, (i, _) in enumerate(items) for j in range(i + 1)]
    item_end = {}
    for t, (n, _) in enumerate(piece_list):
        item_end[n] = t + 1
    state = [dict() for _ in items]
    ring = s_ref.shape[0]
    slot0 = jnp.minimum(pl.program_id(1), 0)

    def blk_rows(j):
        return slice(j * MOBA_BLOCK, (j + 1) * MOBA_BLOCK)

    def score_piece(t):
        n, j = piece_list[t]
        i, hh = items[n]
        sc = jnp.dot(ka_ref[hh, blk_rows(j), :], qa_ref[hh, :, blk_rows(i)],
                     preferred_element_type=F32)
        if j == i:
            sc = jnp.where(causal, sc, MASK_VALUE)
        s_ref[slot0 + t % ring] = sc
        cm = jnp.max(sc, axis=0, keepdims=True)
        st = state[n]
        st["m"] = cm if j == 0 else jnp.maximum(st["m"], cm)

    def softmax_piece(t):
        n, j = piece_list[t]
        i, hh = items[n]
        rows_h = slice(hh * HEAD_DIM, (hh + 1) * HEAD_DIM)
        st = state[n]
        p = jnp.exp2(s_ref[slot0 + t % ring] - st["m"])
        pv = jnp.dot(va_ref[hh, :, blk_rows(j)], p.astype(BF16),
                     preferred_element_type=F32)
        st["acc"] = pv if j == 0 else st["acc"] + pv
        if j == i:
            acc = st["acc"]
            o_ref[0, rows_h, blk_rows(i)] = (
                acc[0:HEAD_DIM] / acc[HEAD_DIM:HEAD_DIM + 1]).astype(BF16)

    total = len(piece_list)
    scored = softmaxed = 0
    while softmaxed < total:
        if scored < total:
            assert scored - softmaxed < ring, "score ring too small for the softmax lag"
            score_piece(scored)
            scored += 1
        ready = min(total, item_end[piece_list[softmaxed][0]] + ATTN_LEAD)
        if scored >= ready:
            softmax_piece(softmaxed)
            softmaxed += 1


def _attention(qt, k, vt):
    bsz, w, s = qt.shape
    blk = jnp.arange(s, dtype=jnp.int32) // MOBA_BLOCK
    lane = jnp.arange(LANES, dtype=jnp.int32)
    oh = jnp.stack([(lane[None, :] == (HEAD_DIM - hh * HEAD_DIM) + blk[:, None])
                    for hh in range(HEADS_PER_STEP)]).astype(BF16)
    avg = jnp.where(jnp.arange(BF16_ROWS, dtype=jnp.int32)[:, None] == blk[None, :],
                    1.0 / MOBA_BLOCK, 0.0).astype(BF16)
    pair_fm = pl.BlockSpec((1, LANES, s), lambda b, p: (b, p, 0))
    return pl.pallas_call(
        _attn_kernel,
        grid=(bsz, ATTN_HEADS // HEADS_PER_STEP),
        in_specs=[
            pair_fm,
            pl.BlockSpec((1, s, LANES), lambda b, p: (b, 0, p)),
            pair_fm,
            pl.BlockSpec((HEADS_PER_STEP, s, LANES), lambda b, p: (0, 0, 0)),
            pl.BlockSpec((BF16_ROWS, s), lambda b, p: (0, 0)),
        ],
        out_specs=pair_fm,
        out_shape=jax.ShapeDtypeStruct((bsz, w, s), BF16),
        scratch_shapes=[pltpu.VMEM((HEADS_PER_STEP, s, LANES), BF16),
                        pltpu.VMEM((HEADS_PER_STEP, LANES, s), BF16),
                        pltpu.VMEM((HEADS_PER_STEP, HEAD_DIM + BF16_ROWS, s), BF16),
                        pltpu.VMEM((ATTN_RING, MOBA_BLOCK, MOBA_BLOCK), F32)],
        compiler_params=pltpu.CompilerParams(
            dimension_semantics=("arbitrary", "arbitrary"), vmem_limit_bytes=VMEM_LIMIT),
        name="moba_attention",
    )(qt, k, vt, oh, avg)


def _outproj_kernel(at_ref, zat_ref, yl_ref, x_ref, mod_ref, ga_ref, wo_ref, fg_ref, o_ref,
                    *, final_norm):
    d = x_ref.shape[2]
    ts = x_ref.shape[1]
    w = ATTN_WIDTH
    gate = mod_ref[0, :, 2 * d:3 * d]
    tn = (((0,), (0,)), ((), ()))
    subs = [slice(r, r + OUT_SUB) for r in range(0, ts, OUT_SUB)]
    ya, y = {}, {}

    def prep(i):
        at = at_ref[0, :, subs[i]].astype(F32)
        rs = lax.rsqrt(jnp.mean(at * at, axis=0, keepdims=True) + EPS)
        z = zat_ref[0, :, subs[i]].astype(F32)
        ya[i] = ((at * rs) * ga_ref[...] * (z * _sigmoid(z))).astype(BF16)

    def project(i):
        yy = lax.dot_general(ya[i], wo_ref[0:w, :], tn, preferred_element_type=F32)
        y[i] = yy + jnp.dot(yl_ref[0, subs[i], :], wo_ref[w:2 * w, :],
                            preferred_element_type=F32)

    def finish(i):
        xo = x_ref[0, subs[i], :] + gate * y[i]
        if final_norm:
            xo = (xo * lax.rsqrt(jnp.mean(xo * xo, axis=-1, keepdims=True) + EPS)) * fg_ref[...]
        o_ref[0, subs[i], :] = xo

    n = len(subs)
    prep(0)
    for i in range(n):
        if i + 1 < n:
            prep(i + 1)
        project(i)
        if i > 0:
            finish(i - 1)
    finish(n - 1)


def _outproj(at, zat, yl, x, mod3, gain_a, w_out, final_gain, ts, final_norm):
    bsz, s, d = x.shape
    w = ATTN_WIDTH
    return pl.pallas_call(
        functools.partial(_outproj_kernel, final_norm=final_norm),
        grid=(bsz, s // ts),
        in_specs=[
            pl.BlockSpec((1, w, ts), lambda b, i: (b, 0, i)),
            pl.BlockSpec((1, w, ts), lambda b, i: (b, 0, i)),
            pl.BlockSpec((1, ts, w), lambda b, i: (b, i, 0)),
            pl.BlockSpec((1, ts, d), lambda b, i: (b, i, 0)),
            pl.BlockSpec((1, 1, 3 * d), lambda b, i: (b, 0, 0)),
            pl.BlockSpec((w, 1), lambda b, i: (0, 0)),
            pl.BlockSpec((2 * w, d), lambda b, i: (0, 0)),
            pl.BlockSpec((1, d), lambda b, i: (0, 0)),
        ],
        out_specs=pl.BlockSpec((1, ts, d), lambda b, i: (b, i, 0)),
        out_shape=jax.ShapeDtypeStruct((bsz, s, d), F32),
        compiler_params=pltpu.CompilerParams(
            dimension_semantics=("arbitrary", "arbitrary"), vmem_limit_bytes=VMEM_LIMIT),
        name="outproj",
    )(at, zat, yl, x, mod3, gain_a, w_out, final_gain)


def _rope_tables(s):
    pos = jnp.arange(s, dtype=F32)
    inv_freq = ROPE_THETA ** (-jnp.arange(0, HEAD_DIM, 2, dtype=F32) / HEAD_DIM)
    ang = pos[:, None] * inv_freq[None, :]
    cos, sin = jnp.cos(ang), jnp.sin(ang)
    cosn = jnp.tile(cos, (1, LANES // HALF_DIM))
    sinn = jnp.tile(jnp.concatenate([-sin, sin], axis=1), (1, LANES // HEAD_DIM))
    cost = jnp.concatenate([cos.T, cos.T], axis=0)
    sint = jnp.concatenate([-sin.T, sin.T], axis=0)
    return cosn, sinn, cost, sint


def _block_diag(wg):
    g, n, _ = wg.shape
    eye = jnp.eye(g, dtype=wg.dtype)
    return jnp.einsum("gij,gh->gihj", wg, eye).reshape(g * n, g * n)


def kernel(x, c, w_mod, b_mod, norm_gain, w_in, conv_w, conv_b, w_rgate, b_rgate,
           w_igate, b_igate, lru_lambda, attn_out_gain, lru_out_gain, w_out, final_gain):
    bsz, s, d = x.shape
    depth = w_in.shape[0]
    w = ATTN_WIDTH
    assert s % OUT_TILE == 0 and d % LANES == 0 and w_in.shape[2] == 6 * w
    cosn, sinn, cost, sint = _rope_tables(s)
    for l in range(depth):
        mod3 = _modulation(c, w_mod[l], b_mod[l]).reshape(bsz, 1, 3 * d)
        wb = w_in[l].astype(BF16)
        w_t = jnp.concatenate([wb[:, 0:w], wb[:, 2 * w:4 * w]], axis=1).T
        qt, vt, zat, k, yl = _inproj_lru(
            x, mod3, norm_gain[l].reshape(1, d), wb, w_t, cosn, sinn, cost, sint,
            conv_w[l], conv_b[l],
            _block_diag(w_rgate[l].astype(BF16)), _block_diag(w_igate[l].astype(BF16)),
            b_rgate[l].reshape(-1), b_igate[l].reshape(-1), lru_lambda[l], lru_out_gain[l],
            IN_TILE)
        at = _attention(qt, k, vt)
        x = _outproj(at, zat, yl, x, mod3, attn_out_gain[l].reshape(w, 1),
                     w_out[l].astype(BF16), final_gain.reshape(1, d), OUT_TILE,
                     final_norm=(l == depth - 1))
    return x
```

```python
import functools

import jax
import jax.numpy as jnp
from jax import lax
from jax.experimental import pallas as pl
from jax.experimental.pallas import tpu as pltpu

ATTN_HEADS = 8
HEAD_DIM = 64
HALF_DIM = HEAD_DIM // 2
ATTN_WIDTH = ATTN_HEADS * HEAD_DIM
LRU_BLOCKS = 8
MOBA_BLOCK = 256
MOBA_TOPK = 3
CONV_WIDTH = 4
LRU_C = 8.0
ROPE_THETA = 10000.0
EPS = 1e-6

LANES = 128
SUBLANES = 8
BF16_ROWS = 16
MASK_VALUE = -1e30
LOG2_E = 1.4426950408889634
VMEM_LIMIT = 56 * 1024 * 1024

F32 = jnp.float32
BF16 = jnp.bfloat16


def _sigmoid(z):
    return 1.0 / (1.0 + jnp.exp(-z))


def _split3(a):
    hi = a.astype(BF16)
    r1 = a - hi.astype(F32)
    mid = r1.astype(BF16)
    lo = (r1 - mid.astype(F32)).astype(BF16)
    return hi, mid, lo


def _mod_kernel(c_ref, w_ref, b_ref, o_ref):
    c = c_ref[...]
    act = c * _sigmoid(c)
    a_hi, a_mid, _ = _split3(act)
    w_hi, w_mid, _ = _split3(w_ref[...])
    dot = functools.partial(jnp.dot, preferred_element_type=F32)
    acc = dot(a_hi, w_hi) + dot(a_hi, w_mid) + dot(a_mid, w_hi)
    o_ref[...] = acc + b_ref[...]


def _modulation(c, w_mod, b_mod):
    bsz, d = c.shape
    n = w_mod.shape[1]
    tn = 512
    return pl.pallas_call(
        _mod_kernel,
        grid=(n // tn,),
        in_specs=[
            pl.BlockSpec((bsz, d), lambda j: (0, 0)),
            pl.BlockSpec((d, tn), lambda j: (0, j)),
            pl.BlockSpec((1, tn), lambda j: (0, j)),
        ],
        out_specs=pl.BlockSpec((bsz, tn), lambda j: (0, j)),
        out_shape=jax.ShapeDtypeStruct((bsz, n), F32),
        compiler_params=pltpu.CompilerParams(
            dimension_semantics=("arbitrary",), vmem_limit_bytes=VMEM_LIMIT),
        name="modulation",
    )(c, w_mod, b_mod.reshape(1, n))


IN_TILE = 512
OUT_TILE = 1024
OUT_SUB = 256
K_COL = 1 * ATTN_WIDTH
XL_COL = 4 * ATTN_WIDTH
ZL_COL = 5 * ATTN_WIDTH
LRU_CHUNK = 128
PROJ_PIECE = 256
LRU_QUAD = 256
LRU_SEGS = SUBLANES
LRU_PITCH = 72


def _inproj_lru_kernel(x_ref, mod_ref, gain_ref, wn_ref, wt_ref, cosn_ref, sinn_ref,
                       cost_ref, sint_ref, cw_ref, cb_ref, wr_ref, wi_ref, br_ref, bi_ref,
                       lam_ref, lg_ref,
                       qt_ref, vt_ref, zat_ref, k_ref, yl_ref,
                       ext_buf, zl_buf, as_ref, us_ref, hc_ref):
    d = x_ref.shape[2]
    ts = x_ref.shape[1]
    w = ATTN_WIDTH
    pad = SUBLANES
    z0 = jnp.minimum(pl.program_id(1), 0)
    ext_ref, zl_ref = ext_buf.at[z0], zl_buf.at[z0]

    @pl.when(pl.program_id(1) == 0)
    def _():
        ext_ref[0:pad, :] = jnp.zeros((pad, w), F32)
        hc_ref[...] = jnp.zeros(hc_ref.shape, F32)

    x = x_ref[0]
    shift = mod_ref[0, :, 0:d]
    scale = mod_ref[0, :, d:2 * d]
    rs = lax.rsqrt(jnp.mean(x * x, axis=-1, keepdims=True) + EPS)
    h = (x * rs) * (gain_ref[...] * (1.0 + scale)) + shift
    hb = h.astype(BF16)

    ext_ref[pad:pad + ts, :] = jnp.dot(hb, wn_ref[:, XL_COL:XL_COL + w],
                                       preferred_element_type=F32)
    zl_ref[...] = jnp.dot(hb, wn_ref[:, ZL_COL:ZL_COL + w], preferred_element_type=F32)

    nt = (((1,), (1,)), ((), ()))
    half = PROJ_PIECE
    n_piece = w // PROJ_PIECE

    def k_piece(p):
        kk = jnp.dot(hb, wn_ref[:, K_COL + p * half:K_COL + (p + 1) * half],
                     preferred_element_type=F32)
        cosn = cosn_ref[...]
        sinn = sinn_ref[...]
        lane = lax.broadcasted_iota(jnp.int32, cosn.shape, 1)
        first_half = (lane % HEAD_DIM) < HALF_DIM
        for g in range(half // LANES):
            kg = kk[:, g * LANES:(g + 1) * LANES]
            swapped = jnp.where(first_half,
                                pltpu.roll(kg, LANES - HALF_DIM, axis=1),
                                pltpu.roll(kg, HALF_DIM, axis=1))
            lo = p * half + g * LANES
            k_ref[0, :, lo:lo + LANES] = (kg * cosn + swapped * sinn).astype(BF16)

    def q_piece(p):
        qt = lax.dot_general(wt_ref[p * half:(p + 1) * half, :], hb, nt,
                             preferred_element_type=F32)
        cost = cost_ref[...]
        sint = sint_ref[...]
        qscale = HEAD_DIM ** -0.5 * LOG2_E
        for hd in range(half // HEAD_DIM):
            t = qt[hd * HEAD_DIM:(hd + 1) * HEAD_DIM]
            swapped = jnp.concatenate([t[HALF_DIM:], t[:HALF_DIM]], axis=0)
            lo = p * half + hd * HEAD_DIM
            qt_ref[0, lo:lo + HEAD_DIM, :] = ((t * cost + swapped * sint) * qscale).astype(BF16)

    def fm_piece(out_ref, base, p):
        lo = p * half
        out_ref[0, lo:lo + half, :] = lax.dot_general(
            wt_ref[base + lo:base + lo + half, :], hb, nt,
            preferred_element_type=F32).astype(BF16)

    lam = lam_ref[...]
    neg_lam = -lam
    softplus = jnp.maximum(neg_lam, 0.0) + jnp.log1p(jnp.exp(-jnp.abs(neg_lam)))
    decay = (-LRU_C * LOG2_E) * softplus
    state = {"h": hc_ref[...]}
    seg_len = ts // LRU_SEGS

    def seg_rows(seg):
        return slice(seg * LRU_PITCH, seg * LRU_PITCH + seg_len)

    def quad_dot(xb, w_ref):
        return jnp.concatenate(
            [jnp.dot(xb[:, q:q + LRU_QUAD], w_ref[q:q + LRU_QUAD, q:q + LRU_QUAD],
                     preferred_element_type=F32) for q in range(0, w, LRU_QUAD)], axis=1)

    def lru_gates(c):
        base = pad + c * LRU_CHUNK
        xc = cb_ref[...] + cw_ref[CONV_WIDTH - 1:CONV_WIDTH, :] * ext_ref[base:base + LRU_CHUNK, :]
        for tap in range(1, CONV_WIDTH):
            xc = xc + (cw_ref[CONV_WIDTH - 1 - tap:CONV_WIDTH - tap, :] *
                       ext_ref[base - tap:base - tap + LRU_CHUNK, :])
        xcb = xc.astype(BF16)
        r = _sigmoid(quad_dot(xcb, wr_ref) + br_ref[...])
        ig = _sigmoid(quad_dot(xcb, wi_ref) + bi_ref[...])
        a = jnp.exp2(decay * r)
        y = 1.0 - a * a
        u = jnp.where(y > 0.0, y * lax.rsqrt(y), 0.0) * ig * xc
        for sl in range(LRU_CHUNK // seg_len):
            seg = c * (LRU_CHUNK // seg_len) + sl
            rows = slice(sl * seg_len, (sl + 1) * seg_len)
            for ln in range(w // LANES):
                lanes = slice(ln * LANES, (ln + 1) * LANES)
                as_ref[ln, seg_rows(seg), :] = a[rows, lanes]
                us_ref[ln, seg_rows(seg), :] = u[rows, lanes]

    def lru_scan():
        hs = [jnp.zeros((LRU_SEGS, LANES), F32) for _ in range(w // LANES)]
        ps = [jnp.ones((LRU_SEGS, LANES), F32) for _ in range(w // LANES)]
        for tau in range(seg_len):
            idx = pl.ds(tau, LRU_SEGS, stride=LRU_PITCH)
            for ln in range(w // LANES):
                av = as_ref[ln, idx, :]
                hs[ln] = av * hs[ln] + us_ref[ln, idx, :]
                ps[ln] = av * ps[ln]
                us_ref[ln, idx, :] = hs[ln]
                as_ref[ln, idx, :] = ps[ln]
        hprev = state["h"]
        h_out = []
        for ln in range(w // LANES):
            ck = hprev[:, ln * LANES:(ln + 1) * LANES]
            for k in range(LRU_SEGS):
                state["carry", ln, k] = ck
                ck = hs[ln][k:k + 1, :] + ps[ln][k:k + 1, :] * ck
            h_out.append(ck)
        state["h"] = jnp.concatenate(h_out, axis=1)

    def lru_finish(c):
        segs = []
        for sl in range(LRU_CHUNK // seg_len):
            seg = c * (LRU_CHUNK // seg_len) + sl
            segs.append(jnp.concatenate(
                [us_ref[ln, seg_rows(seg), :] + as_ref[ln, seg_rows(seg), :] * state["carry", ln, seg]
                 for ln in range(w // LANES)], axis=1))
        rec = jnp.concatenate(segs, axis=0)
        rows = slice(c * LRU_CHUNK, (c + 1) * LRU_CHUNK)
        rsq = lax.rsqrt(jnp.mean(rec * rec, axis=-1, keepdims=True) + EPS)
        z = zl_ref[rows, :]
        yl_ref[0, rows, :] = ((rec * rsq) * lg_ref[...] * (z * _sigmoid(z))).astype(BF16)

    mxu_pieces = ([functools.partial(k_piece, p) for p in range(n_piece)] +
                  [functools.partial(q_piece, p) for p in range(n_piece)] +
                  [functools.partial(fm_piece, vt_ref, w, p) for p in range(n_piece)] +
                  [functools.partial(fm_piece, zat_ref, 2 * w, p) for p in range(n_piece)])
    lru_pieces = ([functools.partial(lru_gates, c) for c in range(ts // LRU_CHUNK)] + [lru_scan] +
                  [functools.partial(lru_finish, c) for c in range(ts // LRU_CHUNK)])
    done = 0
    for i, piece in enumerate(mxu_pieces):
        piece()
        upto = (i + 1) * len(lru_pieces) // len(mxu_pieces)
        for lp in lru_pieces[done:upto]:
            lp()
        done = upto

    hc_ref[...] = state["h"]
    ext_ref[0:pad, :] = ext_ref[ts:ts + pad, :]


def _inproj_lru(x, mod3, gain, w_nat, w_t, cosn, sinn, cost, sint,
                conv_w, conv_b, wr_bd, wi_bd, b_r, b_i, lam, lru_gain, ts):
    bsz, s, d = x.shape
    w = ATTN_WIDTH
    fm = jax.ShapeDtypeStruct((bsz, w, s), BF16)
    sm = jax.ShapeDtypeStruct((bsz, s, w), BF16)
    fm_spec = pl.BlockSpec((1, w, ts), lambda b, i: (b, 0, i))
    sm_spec = pl.BlockSpec((1, ts, w), lambda b, i: (b, i, 0))
    vec_spec = pl.BlockSpec((1, w), lambda b, i: (0, 0))
    mat_spec = pl.BlockSpec((w, w), lambda b, i: (0, 0))
    row = lambda a: a.reshape(1, w)
    return pl.pallas_call(
        _inproj_lru_kernel,
        grid=(bsz, s // ts),
        in_specs=[
            pl.BlockSpec((1, ts, d), lambda b, i: (b, i, 0)),
            pl.BlockSpec((1, 1, 3 * d), lambda b, i: (b, 0, 0)),
            pl.BlockSpec((1, d), lambda b, i: (0, 0)),
            pl.BlockSpec((d, 6 * w), lambda b, i: (0, 0)),
            pl.BlockSpec((3 * w, d), lambda b, i: (0, 0)),
            pl.BlockSpec((ts, LANES), lambda b, i: (i, 0)),
            pl.BlockSpec((ts, LANES), lambda b, i: (i, 0)),
            pl.BlockSpec((HEAD_DIM, ts), lambda b, i: (0, i)),
            pl.BlockSpec((HEAD_DIM, ts), lambda b, i: (0, i)),
            pl.BlockSpec((CONV_WIDTH, w), lambda b, i: (0, 0)), vec_spec,
            mat_spec, mat_spec, vec_spec, vec_spec, vec_spec, vec_spec,
        ],
        out_specs=[fm_spec, fm_spec, fm_spec, sm_spec, sm_spec],
        out_shape=[fm, fm, fm, sm, sm],
        scratch_shapes=[pltpu.VMEM((1, ts + SUBLANES, w), F32),
                        pltpu.VMEM((1, ts, w), F32),
                        pltpu.VMEM((w // LANES, LRU_SEGS * LRU_PITCH, LANES), F32),
                        pltpu.VMEM((w // LANES, LRU_SEGS * LRU_PITCH, LANES), F32),
                        pltpu.VMEM((1, w), F32)],
        compiler_params=pltpu.CompilerParams(
            dimension_semantics=("arbitrary", "arbitrary"), vmem_limit_bytes=VMEM_LIMIT),
        name="inproj_rglru",
    )(x, mod3, gain, w_nat, w_t, cosn, sinn, cost, sint,
      conv_w, row(conv_b), wr_bd, wi_bd, row(b_r), row(b_i), row(lam), row(lru_gain))


HEADS_PER_GROUP = LANES // HEAD_DIM
HEADS_PER_STEP = 4
BLOCK_SHIFT = MOBA_BLOCK.bit_length() - 1
assert 1 << BLOCK_SHIFT == MOBA_BLOCK
ATTN_LEAD = 5
ATTN_RING = 16


def _attn_kernel(qt_ref, k_ref, vt_ref, oh_ref, avg_ref, o_ref, ka_ref, qa_ref, va_ref, s_ref):
    s = qt_ref.shape[2]
    nblk = s // MOBA_BLOCK

    jrow = lax.broadcasted_iota(jnp.int32, (nblk, s), 0)
    qblk = lax.shift_right_logical(lax.broadcasted_iota(jnp.int32, (nblk, s), 1), BLOCK_SHIFT)
    cand = jrow < qblk
    qb = qblk[0:1, :]
    kr = lax.broadcasted_iota(jnp.int32, (MOBA_BLOCK, MOBA_BLOCK), 0)
    qc = lax.broadcasted_iota(jnp.int32, (MOBA_BLOCK, MOBA_BLOCK), 1)
    causal = kr <= qc

    for hh in range(HEADS_PER_STEP):
        grp, hp = divmod(hh, HEADS_PER_GROUP)
        grp_lanes = slice(grp * LANES, (grp + 1) * LANES)
        head_lo = hp * HEAD_DIM
        aux_lo = HEAD_DIM - head_lo
        rows_l = slice(head_lo, head_lo + HEAD_DIM)
        rows_h = slice(hh * HEAD_DIM, (hh + 1) * HEAD_DIM)

        if hp == 0:
            kmean = jnp.dot(avg_ref[...], k_ref[0, :, grp_lanes], preferred_element_type=F32)
            km3 = jnp.concatenate(_split3(kmean), axis=0)

        ka_ref[hh] = oh_ref[hp]
        ka_ref[hh, :, rows_l] = k_ref[0, :, grp * LANES + head_lo:grp * LANES + head_lo + HEAD_DIM]

        va_ref[hh, 0:HEAD_DIM, :] = vt_ref[0, rows_h, :]
        va_ref[hh, HEAD_DIM:, :] = jnp.ones((BF16_ROWS, s), BF16)

        qa_ref[hh] = jnp.zeros(qa_ref.shape[1:], BF16)
        qa_ref[hh, rows_l, :] = qt_ref[0, rows_h, :]
        g3 = jnp.dot(km3, qa_ref[hh], preferred_element_type=F32)
        gate = (g3[0:BF16_ROWS] + g3[BF16_ROWS:2 * BF16_ROWS]) + g3[2 * BF16_ROWS:]
        gate = gate[0:nblk]

        bias_rows = []
        for j in range(nblk):
            gj = gate[j:j + 1, :]
            beats = cand & ((gate > gj) | ((gate == gj) & (jrow < j)))
            rank = jnp.sum(beats.astype(F32), axis=0, keepdims=True)
            keep = ((qb > j) & (rank < MOBA_TOPK)) | (qb == j)
            bias_rows.append(jnp.where(keep, 0.0, MASK_VALUE))
        bias = jnp.concatenate(bias_rows + [jnp.zeros((BF16_ROWS - nblk, s), F32)], axis=0)
        qa_ref[hh, aux_lo:aux_lo + BF16_ROWS, :] = bias.astype(BF16)

    items = [(i, hh) for hh in range(HEADS_PER_STEP)
             for i in (range(nblk) if hh % 2 == 0 else reversed(range(nblk)))]
    piece_list = [(n, j) for n, (i, _) in enumerate(items) for j in range(i + 1)]
    item_end = {}
    for t, (n, _) in enumerate(piece_list):
        item_end[n] = t + 1
    state = [dict() for _ in items]
    ring = s_ref.shape[0]
    slot0 = jnp.minimum(pl.program_id(1), 0)

    def blk_rows(j):
        return slice(j * MOBA_BLOCK, (j + 1) * MOBA_BLOCK)

    def score_piece(t):
        n, j = piece_list[t]
        i, hh = items[n]
        sc = jnp.dot(ka_ref[hh, blk_rows(j), :], qa_ref[hh, :, blk_rows(i)],
                     preferred_element_type=F32)
        if j == i:
            sc = jnp.where(causal, sc, MASK_VALUE)
        s_ref[slot0 + t % ring] = sc
        cm = jnp.max(sc, axis=0, keepdims=True)
        st = state[n]
        st["m"] = cm if j == 0 else jnp.maximum(st["m"], cm)

    def softmax_piece(t):
        n, j = piece_list[t]
        i, hh = items[n]
        rows_h = slice(hh * HEAD_DIM, (hh + 1) * HEAD_DIM)
        st = state[n]
        p = jnp.exp2(s_ref[slot0 + t % ring] - st["m"])
        pv = jnp.dot(va_ref[hh, :, blk_rows(j)], p.astype(BF16),
                     preferred_element_type=F32)
        st["acc"] = pv if j == 0 else st["acc"] + pv
        if j == i:
            acc = st["acc"]
            o_ref[0, rows_h, blk_rows(i)] = (
                acc[0:HEAD_DIM] / acc[HEAD_DIM:HEAD_DIM + 1]).astype(BF16)

    total = len(piece_list)
    scored = softmaxed = 0
    while softmaxed < total:
        if scored < total:
            assert scored - softmaxed < ring, "score ring too small for the softmax lag"
            score_piece(scored)
            scored += 1
        ready = min(total, item_end[piece_list[softmaxed][0]] + ATTN_LEAD)
        if scored >= ready:
            softmax_piece(softmaxed)
            softmaxed += 1


def _attention(qt, k, vt):
    bsz, w, s = qt.shape
    blk = jnp.arange(s, dtype=jnp.int32) // MOBA_BLOCK
    lane = jnp.arange(LANES, dtype=jnp.int32)
    oh = jnp.stack([(lane[None, :] == (HEAD_DIM - hp * HEAD_DIM) + blk[:, None])
                    for hp in range(HEADS_PER_GROUP)]).astype(BF16)
    avg = jnp.where(jnp.arange(BF16_ROWS, dtype=jnp.int32)[:, None] == blk[None, :],
                    1.0 / MOBA_BLOCK, 0.0).astype(BF16)
    step_w = HEADS_PER_STEP * HEAD_DIM
    pair_fm = pl.BlockSpec((1, step_w, s), lambda b, p: (b, p, 0))
    return pl.pallas_call(
        _attn_kernel,
        grid=(bsz, ATTN_HEADS // HEADS_PER_STEP),
        in_specs=[
            pair_fm,
            pl.BlockSpec((1, s, step_w), lambda b, p: (b, 0, p)),
            pair_fm,
            pl.BlockSpec((HEADS_PER_GROUP, s, LANES), lambda b, p: (0, 0, 0)),
            pl.BlockSpec((BF16_ROWS, s), lambda b, p: (0, 0)),
        ],
        out_specs=pair_fm,
        out_shape=jax.ShapeDtypeStruct((bsz, w, s), BF16),
        scratch_shapes=[pltpu.VMEM((HEADS_PER_STEP, s, LANES), BF16),
                        pltpu.VMEM((HEADS_PER_STEP, LANES, s), BF16),
                        pltpu.VMEM((HEADS_PER_STEP, HEAD_DIM + BF16_ROWS, s), BF16),
                        pltpu.VMEM((ATTN_RING, MOBA_BLOCK, MOBA_BLOCK), F32)],
        compiler_params=pltpu.CompilerParams(
            dimension_semantics=("arbitrary", "arbitrary"), vmem_limit_bytes=VMEM_LIMIT),
        name="moba_attention",
    )(qt, k, vt, oh, avg)


def _outproj_kernel(at_ref, zat_ref, yl_ref, x_ref, mod_ref, ga_ref, wo_ref, fg_ref, o_ref,
                    *, final_norm):
    d = x_ref.shape[2]
    ts = x_ref.shape[1]
    w = ATTN_WIDTH
    gate = mod_ref[0, :, 2 * d:3 * d]
    tn = (((0,), (0,)), ((), ()))
    subs = [slice(r, r + OUT_SUB) for r in range(0, ts, OUT_SUB)]
    ya, y = {}, {}

    def prep(i):
        at = at_ref[0, :, subs[i]].astype(F32)
        rs = lax.rsqrt(jnp.mean(at * at, axis=0, keepdims=True) + EPS)
        z = zat_ref[0, :, subs[i]].astype(F32)
        ya[i] = ((at * rs) * ga_ref[...] * (z * _sigmoid(z))).astype(BF16)

    def project(i):
        yy = lax.dot_general(ya[i], wo_ref[0:w, :], tn, preferred_element_type=F32)
        y[i] = yy + jnp.dot(yl_ref[0, subs[i], :], wo_ref[w:2 * w, :],
                            preferred_element_type=F32)

    def finish(i):
        xo = x_ref[0, subs[i], :] + gate * y[i]
        if final_norm:
            xo = (xo * lax.rsqrt(jnp.mean(xo * xo, axis=-1, keepdims=True) + EPS)) * fg_ref[...]
        o_ref[0, subs[i], :] = xo

    n = len(subs)
    prep(0)
    for i in range(n):
        if i + 1 < n:
            prep(i + 1)
        project(i)
        if i > 0:
            finish(i - 1)
    finish(n - 1)


def _outproj(at, zat, yl, x, mod3, gain_a, w_out, final_gain, ts, final_norm):
    bsz, s, d = x.shape
    w = ATTN_WIDTH
    return pl.pallas_call(
        functools.partial(_outproj_kernel, final_norm=final_norm),
        grid=(bsz, s // ts),
        in_specs=[
            pl.BlockSpec((1, w, ts), lambda b, i: (b, 0, i)),
            pl.BlockSpec((1, w, ts), lambda b, i: (b, 0, i)),
            pl.BlockSpec((1, ts, w), lambda b, i: (b, i, 0)),
            pl.BlockSpec((1, ts, d), lambda b, i: (b, i, 0)),
            pl.BlockSpec((1, 1, 3 * d), lambda b, i: (b, 0, 0)),
            pl.BlockSpec((w, 1), lambda b, i: (0, 0)),
            pl.BlockSpec((2 * w, d), lambda b, i: (0, 0)),
            pl.BlockSpec((1, d), lambda b, i: (0, 0)),
        ],
        out_specs=pl.BlockSpec((1, ts, d), lambda b, i: (b, i, 0)),
        out_shape=jax.ShapeDtypeStruct((bsz, s, d), F32),
        compiler_params=pltpu.CompilerParams(
            dimension_semantics=("arbitrary", "arbitrary"), vmem_limit_bytes=VMEM_LIMIT),
        name="outproj",
    )(at, zat, yl, x, mod3, gain_a, w_out, final_gain)


def _rope_tables(s):
    pos = jnp.arange(s, dtype=F32)
    inv_freq = ROPE_THETA ** (-jnp.arange(0, HEAD_DIM, 2, dtype=F32) / HEAD_DIM)
    ang = pos[:, None] * inv_freq[None, :]
    cos, sin = jnp.cos(ang), jnp.sin(ang)
    cosn = jnp.tile(cos, (1, LANES // HALF_DIM))
    sinn = jnp.tile(jnp.concatenate([-sin, sin], axis=1), (1, LANES // HEAD_DIM))
    cost = jnp.concatenate([cos.T, cos.T], axis=0)
    sint = jnp.concatenate([-sin.T, sin.T], axis=0)
    return cosn, sinn, cost, sint


def _block_diag(wg):
    g, n, _ = wg.shape
    eye = jnp.eye(g, dtype=wg.dtype)
    return jnp.einsum("gij,gh->gihj", wg, eye).reshape(g * n, g * n)


def kernel(x, c, w_mod, b_mod, norm_gain, w_in, conv_w, conv_b, w_rgate, b_rgate,
           w_igate, b_igate, lru_lambda, attn_out_gain, lru_out_gain, w_out, final_gain):
    bsz, s, d = x.shape
    depth = w_in.shape[0]
    w = ATTN_WIDTH
    assert s % OUT_TILE == 0 and d % LANES == 0 and w_in.shape[2] == 6 * w
    cosn, sinn, cost, sint = _rope_tables(s)
    for l in range(depth):
        mod3 = _modulation(c, w_mod[l], b_mod[l]).reshape(bsz, 1, 3 * d)
        wb = w_in[l].astype(BF16)
        w_t = jnp.concatenate([wb[:, 0:w], wb[:, 2 * w:4 * w]], axis=1).T
        qt, vt, zat, k, yl = _inproj_lru(
            x, mod3, norm_gain[l].reshape(1, d), wb, w_t, cosn, sinn, cost, sint,
            conv_w[l], conv_b[l],
            _block_diag(w_rgate[l].astype(BF16)), _block_diag(w_igate[l].astype(BF16)),
            b_rgate[l].reshape(-1), b_igate[l].reshape(-1), lru_lambda[l], lru_out_gain[l],
            IN_TILE)
        at = _attention(qt, k, vt)
        x = _outproj(at, zat, yl, x, mod3, attn_out_gain[l].reshape(w, 1),
                     w_out[l].astype(BF16), final_gain.reshape(1, d), OUT_TILE,
                     final_norm=(l == depth - 1))
    return x
```

```python
import functools

import jax
import jax.numpy as jnp
from jax import lax
from jax.experimental import pallas as pl
from jax.experimental.pallas import tpu as pltpu

ATTN_HEADS = 8
HEAD_DIM = 64
HALF_DIM = HEAD_DIM // 2
ATTN_WIDTH = ATTN_HEADS * HEAD_DIM
LRU_BLOCKS = 8
MOBA_BLOCK = 256
MOBA_TOPK = 3
CONV_WIDTH = 4
LRU_C = 8.0
ROPE_THETA = 10000.0
EPS = 1e-6

LANES = 128
SUBLANES = 8
BF16_ROWS = 16
MASK_VALUE = -1e30
LOG2_E = 1.4426950408889634
VMEM_LIMIT = 56 * 1024 * 1024

F32 = jnp.float32
BF16 = jnp.bfloat16


def _sigmoid(z):
    return 1.0 / (1.0 + jnp.exp(-z))


def _split3(a):
    hi = a.astype(BF16)
    r1 = a - hi.astype(F32)
    mid = r1.astype(BF16)
    lo = (r1 - mid.astype(F32)).astype(BF16)
    return hi, mid, lo


def _mod_kernel(c_ref, w_ref, b_ref, o_ref):
    c = c_ref[...]
    act = c * _sigmoid(c)
    a_hi, a_mid, _ = _split3(act)
    w_hi, w_mid, _ = _split3(w_ref[...])
    dot = functools.partial(jnp.dot, preferred_element_type=F32)
    acc = dot(a_hi, w_hi) + dot(a_hi, w_mid) + dot(a_mid, w_hi)
    o_ref[...] = acc + b_ref[...]


def _modulation(c, w_mod, b_mod):
    bsz, d = c.shape
    n = w_mod.shape[1]
    tn = 512
    return pl.pallas_call(
        _mod_kernel,
        grid=(n // tn,),
        in_specs=[
            pl.BlockSpec((bsz, d), lambda j: (0, 0)),
            pl.BlockSpec((d, tn), lambda j: (0, j)),
            pl.BlockSpec((1, tn), lambda j: (0, j)),
        ],
        out_specs=pl.BlockSpec((bsz, tn), lambda j: (0, j)),
        out_shape=jax.ShapeDtypeStruct((bsz, n), F32),
        compiler_params=pltpu.CompilerParams(
            dimension_semantics=("arbitrary",), vmem_limit_bytes=VMEM_LIMIT),
        name="modulation",
    )(c, w_mod, b_mod.reshape(1, n))


IN_TILE = 512
OUT_TILE = 2048
OUT_SUB = 256
K_COL = 1 * ATTN_WIDTH
XL_COL = 4 * ATTN_WIDTH
ZL_COL = 5 * ATTN_WIDTH
LRU_CHUNK = 128
PROJ_PIECE = 256
LRU_QUAD = 256


def _inproj_lru_kernel(x_ref, mod_ref, gain_ref, wn_ref, wt_ref, cosn_ref, sinn_ref,
                       cost_ref, sint_ref, cw_ref, cb_ref, wr_ref, wi_ref, br_ref, bi_ref,
                       lam_ref, lg_ref,
                       qt_ref, vt_ref, zat_ref, k_ref, yl_ref,
                       ext_buf, zl_buf, a_buf, u_buf, hc_ref):
    d = x_ref.shape[2]
    ts = x_ref.shape[1]
    w = ATTN_WIDTH
    pad = SUBLANES
    z0 = jnp.minimum(pl.program_id(1), 0)
    ext_ref, zl_ref, a_ref, u_ref = ext_buf.at[z0], zl_buf.at[z0], a_buf.at[z0], u_buf.at[z0]

    @pl.when(pl.program_id(1) == 0)
    def _():
        ext_ref[0:pad, :] = jnp.zeros((pad, w), F32)
        hc_ref[...] = jnp.zeros(hc_ref.shape, F32)

    x = x_ref[0]
    shift = mod_ref[0, :, 0:d]
    scale = mod_ref[0, :, d:2 * d]
    rs = lax.rsqrt(jnp.mean(x * x, axis=-1, keepdims=True) + EPS)
    h = (x * rs) * (gain_ref[...] * (1.0 + scale)) + shift
    hb = h.astype(BF16)

    ext_ref[pad:pad + ts, :] = jnp.dot(hb, wn_ref[:, XL_COL:XL_COL + w],
                                       preferred_element_type=F32)
    zl_ref[...] = jnp.dot(hb, wn_ref[:, ZL_COL:ZL_COL + w], preferred_element_type=F32)

    nt = (((1,), (1,)), ((), ()))
    half = PROJ_PIECE
    n_piece = w // PROJ_PIECE

    def k_piece(p):
        kk = jnp.dot(hb, wn_ref[:, K_COL + p * half:K_COL + (p + 1) * half],
                     preferred_element_type=F32)
        cosn = cosn_ref[...]
        sinn = sinn_ref[...]
        lane = lax.broadcasted_iota(jnp.int32, cosn.shape, 1)
        first_half = (lane % HEAD_DIM) < HALF_DIM
        for g in range(half // LANES):
            kg = kk[:, g * LANES:(g + 1) * LANES]
            swapped = jnp.where(first_half,
                                pltpu.roll(kg, LANES - HALF_DIM, axis=1),
                                pltpu.roll(kg, HALF_DIM, axis=1))
            lo = p * half + g * LANES
            k_ref[0, :, lo:lo + LANES] = (kg * cosn + swapped * sinn).astype(BF16)

    def q_piece(p):
        qt = lax.dot_general(wt_ref[p * half:(p + 1) * half, :], hb, nt,
                             preferred_element_type=F32)
        cost = cost_ref[...]
        sint = sint_ref[...]
        qscale = HEAD_DIM ** -0.5 * LOG2_E
        for hd in range(half // HEAD_DIM):
            t = qt[hd * HEAD_DIM:(hd + 1) * HEAD_DIM]
            swapped = jnp.concatenate([t[HALF_DIM:], t[:HALF_DIM]], axis=0)
            lo = p * half + hd * HEAD_DIM
            qt_ref[0, lo:lo + HEAD_DIM, :] = ((t * cost + swapped * sint) * qscale).astype(BF16)

    def fm_piece(out_ref, base, p):
        lo = p * half
        out_ref[0, lo:lo + half, :] = lax.dot_general(
            wt_ref[base + lo:base + lo + half, :], hb, nt,
            preferred_element_type=F32).astype(BF16)

    lam = lam_ref[...]
    neg_lam = -lam
    softplus = jnp.maximum(neg_lam, 0.0) + jnp.log1p(jnp.exp(-jnp.abs(neg_lam)))
    decay = (-LRU_C * LOG2_E) * softplus
    state = {"h": hc_ref[...]}

    def quad_dot(xb, w_ref):
        return jnp.concatenate(
            [jnp.dot(xb[:, q:q + LRU_QUAD], w_ref[q:q + LRU_QUAD, q:q + LRU_QUAD],
                     preferred_element_type=F32) for q in range(0, w, LRU_QUAD)], axis=1)

    def lru_gates(c):
        base = pad + c * LRU_CHUNK
        xc = cb_ref[...] + cw_ref[CONV_WIDTH - 1:CONV_WIDTH, :] * ext_ref[base:base + LRU_CHUNK, :]
        for tap in range(1, CONV_WIDTH):
            xc = xc + (cw_ref[CONV_WIDTH - 1 - tap:CONV_WIDTH - tap, :] *
                       ext_ref[base - tap:base - tap + LRU_CHUNK, :])
        xcb = xc.astype(BF16)
        r = _sigmoid(quad_dot(xcb, wr_ref) + br_ref[...])
        ig = _sigmoid(quad_dot(xcb, wi_ref) + bi_ref[...])
        a = jnp.exp2(decay * r)
        y = 1.0 - a * a
        u = jnp.where(y > 0.0, y * lax.rsqrt(y), 0.0) * ig * xc
        grp = (LRU_CHUNK // SUBLANES, SUBLANES, w)
        a = a.reshape(grp)
        u = u.reshape(grp)
        rowg = lax.broadcasted_iota(jnp.int32, grp, 1)
        for sh in (1, 2, 4):
            ok = rowg >= sh
            a_prev = pltpu.roll(a, sh, axis=1)
            u_prev = pltpu.roll(u, sh, axis=1)
            u = jnp.where(ok, a * u_prev + u, u)
            a = jnp.where(ok, a * a_prev, a)
        a_ref[c * LRU_CHUNK:(c + 1) * LRU_CHUNK, :] = a.reshape(LRU_CHUNK, w)
        u_ref[c * LRU_CHUNK:(c + 1) * LRU_CHUNK, :] = u.reshape(LRU_CHUNK, w)

    def lru_carry(c):
        hprev = state["h"]
        for g in range(LRU_CHUNK // SUBLANES):
            r0 = c * LRU_CHUNK + g * SUBLANES
            hcur = u_ref[r0:r0 + SUBLANES, :] + a_ref[r0:r0 + SUBLANES, :] * hprev
            u_ref[r0:r0 + SUBLANES, :] = hcur
            hprev = hcur[SUBLANES - 1:SUBLANES, :]
        state["h"] = hprev

    def lru_finish(c):
        rows = slice(c * LRU_CHUNK, (c + 1) * LRU_CHUNK)
        rec = u_ref[rows, :]
        rsq = lax.rsqrt(jnp.mean(rec * rec, axis=-1, keepdims=True) + EPS)
        z = zl_ref[rows, :]
        yl_ref[0, rows, :] = ((rec * rsq) * lg_ref[...] * (z * _sigmoid(z))).astype(BF16)

    mxu_pieces = ([functools.partial(k_piece, p) for p in range(n_piece)] +
                  [functools.partial(q_piece, p) for p in range(n_piece)] +
                  [functools.partial(fm_piece, vt_ref, w, p) for p in range(n_piece)] +
                  [functools.partial(fm_piece, zat_ref, 2 * w, p) for p in range(n_piece)])
    lru_pieces = []
    for c in range(ts // LRU_CHUNK):
        lru_pieces += [functools.partial(lru_gates, c), functools.partial(lru_carry, c),
                       functools.partial(lru_finish, c)]
    done = 0
    for i, piece in enumerate(mxu_pieces):
        piece()
        upto = (i + 1) * len(lru_pieces) // len(mxu_pieces)
        for lp in lru_pieces[done:upto]:
            lp()
        done = upto

    hc_ref[...] = state["h"]
    ext_ref[0:pad, :] = ext_ref[ts:ts + pad, :]


def _inproj_lru(x, mod3, gain, w_nat, w_t, cosn, sinn, cost, sint,
                conv_w, conv_b, wr_bd, wi_bd, b_r, b_i, lam, lru_gain, ts):
    bsz, s, d = x.shape
    w = ATTN_WIDTH
    fm = jax.ShapeDtypeStruct((bsz, w, s), BF16)
    sm = jax.ShapeDtypeStruct((bsz, s, w), BF16)
    fm_spec = pl.BlockSpec((1, w, ts), lambda b, i: (b, 0, i))
    sm_spec = pl.BlockSpec((1, ts, w), lambda b, i: (b, i, 0))
    vec_spec = pl.BlockSpec((1, w), lambda b, i: (0, 0))
    mat_spec = pl.BlockSpec((w, w), lambda b, i: (0, 0))
    row = lambda a: a.reshape(1, w)
    return pl.pallas_call(
        _inproj_lru_kernel,
        grid=(bsz, s // ts),
        in_specs=[
            pl.BlockSpec((1, ts, d), lambda b, i: (b, i, 0)),
            pl.BlockSpec((1, 1, 3 * d), lambda b, i: (b, 0, 0)),
            pl.BlockSpec((1, d), lambda b, i: (0, 0)),
            pl.BlockSpec((d, 6 * w), lambda b, i: (0, 0)),
            pl.BlockSpec((3 * w, d), lambda b, i: (0, 0)),
            pl.BlockSpec((ts, LANES), lambda b, i: (i, 0)),
            pl.BlockSpec((ts, LANES), lambda b, i: (i, 0)),
            pl.BlockSpec((HEAD_DIM, ts), lambda b, i: (0, i)),
            pl.BlockSpec((HEAD_DIM, ts), lambda b, i: (0, i)),
            pl.BlockSpec((CONV_WIDTH, w), lambda b, i: (0, 0)), vec_spec,
            mat_spec, mat_spec, vec_spec, vec_spec, vec_spec, vec_spec,
        ],
        out_specs=[fm_spec, fm_spec, fm_spec, sm_spec, sm_spec],
        out_shape=[fm, fm, fm, sm, sm],
        scratch_shapes=[pltpu.VMEM((1, ts + SUBLANES, w), F32),
                        pltpu.VMEM((1, ts, w), F32),
                        pltpu.VMEM((1, ts, w), F32),
                        pltpu.VMEM((1, ts, w), F32),
                        pltpu.VMEM((1, w), F32)],
        compiler_params=pltpu.CompilerParams(
            dimension_semantics=("arbitrary", "arbitrary"), vmem_limit_bytes=VMEM_LIMIT),
        name="inproj_rglru",
    )(x, mod3, gain, w_nat, w_t, cosn, sinn, cost, sint,
      conv_w, row(conv_b), wr_bd, wi_bd, row(b_r), row(b_i), row(lam), row(lru_gain))


HEADS_PER_GROUP = LANES // HEAD_DIM
HEADS_PER_STEP = 4
BLOCK_SHIFT = MOBA_BLOCK.bit_length() - 1
assert 1 << BLOCK_SHIFT == MOBA_BLOCK
ATTN_LEAD = 5
ATTN_RING = 16


def _attn_kernel(qt_ref, k_ref, vt_ref, oh_ref, avg_ref, o_ref, ka_ref, qa_ref, va_ref, s_ref):
    s = qt_ref.shape[2]
    nblk = s // MOBA_BLOCK

    jrow = lax.broadcasted_iota(jnp.int32, (nblk, s), 0)
    qblk = lax.shift_right_logical(lax.broadcasted_iota(jnp.int32, (nblk, s), 1), BLOCK_SHIFT)
    cand = jrow < qblk
    qb = qblk[0:1, :]
    kr = lax.broadcasted_iota(jnp.int32, (MOBA_BLOCK, MOBA_BLOCK), 0)
    qc = lax.broadcasted_iota(jnp.int32, (MOBA_BLOCK, MOBA_BLOCK), 1)
    causal = kr <= qc

    for hh in range(HEADS_PER_STEP):
        grp, hp = divmod(hh, HEADS_PER_GROUP)
        grp_lanes = slice(grp * LANES, (grp + 1) * LANES)
        head_lo = hp * HEAD_DIM
        aux_lo = HEAD_DIM - head_lo
        rows_l = slice(head_lo, head_lo + HEAD_DIM)
        rows_h = slice(hh * HEAD_DIM, (hh + 1) * HEAD_DIM)

        if hp == 0:
            kmean = jnp.dot(avg_ref[...], k_ref[0, :, grp_lanes], preferred_element_type=F32)
            km3 = jnp.concatenate(_split3(kmean), axis=0)

        ka_ref[hh] = oh_ref[hp]
        ka_ref[hh, :, rows_l] = k_ref[0, :, grp * LANES + head_lo:grp * LANES + head_lo + HEAD_DIM]

        va_ref[hh, 0:HEAD_DIM, :] = vt_ref[0, rows_h, :]
        va_ref[hh, HEAD_DIM:, :] = jnp.ones((BF16_ROWS, s), BF16)

        qa_ref[hh] = jnp.zeros(qa_ref.shape[1:], BF16)
        qa_ref[hh, rows_l, :] = qt_ref[0, rows_h, :]
        g3 = jnp.dot(km3, qa_ref[hh], preferred_element_type=F32)
        gate = (g3[0:BF16_ROWS] + g3[BF16_ROWS:2 * BF16_ROWS]) + g3[2 * BF16_ROWS:]
        gate = gate[0:nblk]

        bias_rows = []
        for j in range(nblk):
            gj = gate[j:j + 1, :]
            beats = cand & ((gate > gj) | ((gate == gj) & (jrow < j)))
            rank = jnp.sum(beats.astype(F32), axis=0, keepdims=True)
            keep = ((qb > j) & (rank < MOBA_TOPK)) | (qb == j)
            bias_rows.append(jnp.where(keep, 0.0, MASK_VALUE))
        bias = jnp.concatenate(bias_rows + [jnp.zeros((BF16_ROWS - nblk, s), F32)], axis=0)
        qa_ref[hh, aux_lo:aux_lo + BF16_ROWS, :] = bias.astype(BF16)

    items = [(i, hh) for hh in range(HEADS_PER_STEP)
             for i in (range(nblk) if hh % 2 == 0 else reversed(range(nblk)))]
    piece_list = [(n, j) for n, (i, _) in enumerate(items) for j in range(i + 1)]
    item_end = {}
    for t, (n, _) in enumerate(piece_list):
        item_end[n] = t + 1
    state = [dict() for _ in items]
    ring = s_ref.shape[0]
    slot0 = jnp.minimum(pl.program_id(1), 0)

    def blk_rows(j):
        return slice(j * MOBA_BLOCK, (j + 1) * MOBA_BLOCK)

    def score_piece(t):
        n, j = piece_list[t]
        i, hh = items[n]
        sc = jnp.dot(ka_ref[hh, blk_rows(j), :], qa_ref[hh, :, blk_rows(i)],
                     preferred_element_type=F32)
        if j == i:
            sc = jnp.where(causal, sc, MASK_VALUE)
        s_ref[slot0 + t % ring] = sc
        cm = jnp.max(sc, axis=0, keepdims=True)
        st = state[n]
        st["m"] = cm if j == 0 else jnp.maximum(st["m"], cm)

    def softmax_piece(t):
        n, j = piece_list[t]
        i, hh = items[n]
        rows_h = slice(hh * HEAD_DIM, (hh + 1) * HEAD_DIM)
        st = state[n]
        p = jnp.exp2(s_ref[slot0 + t % ring] - st["m"])
        pv = jnp.dot(va_ref[hh, :, blk_rows(j)], p.astype(BF16),
                     preferred_element_type=F32)
        st["acc"] = pv if j == 0 else st["acc"] + pv
        if j == i:
            acc = st["acc"]
            o_ref[0, rows_h, blk_rows(i)] = (
                acc[0:HEAD_DIM] / acc[HEAD_DIM:HEAD_DIM + 1]).astype(BF16)

    total = len(piece_list)
    scored = softmaxed = 0
    while softmaxed < total:
        if scored < total:
            assert scored - softmaxed < ring, "score ring too small for the softmax lag"
            score_piece(scored)
            scored += 1
        ready = min(total, item_end[piece_list[softmaxed][0]] + ATTN_LEAD)
        if scored >= ready:
            softmax_piece(softmaxed)
            softmaxed += 1


def _attention(qt, k, vt):
    bsz, w, s = qt.shape
    blk = jnp.arange(s, dtype=jnp.int32) // MOBA_BLOCK
    lane = jnp.arange(LANES, dtype=jnp.int32)
    oh = jnp.stack([(lane[None, :] == (HEAD_DIM - hp * HEAD_DIM) + blk[:, None])
                    for hp in range(HEADS_PER_GROUP)]).astype(BF16)
    avg = jnp.where(jnp.arange(BF16_ROWS, dtype=jnp.int32)[:, None] == blk[None, :],
                    1.0 / MOBA_BLOCK, 0.0).astype(BF16)
    step_w = HEADS_PER_STEP * HEAD_DIM
    pair_fm = pl.BlockSpec((1, step_w, s), lambda b, p: (b, p, 0))
    return pl.pallas_call(
        _attn_kernel,
        grid=(bsz, ATTN_HEADS // HEADS_PER_STEP),
        in_specs=[
            pair_fm,
            pl.BlockSpec((1, s, step_w), lambda b, p: (b, 0, p)),
            pair_fm,
            pl.BlockSpec((HEADS_PER_GROUP, s, LANES), lambda b, p: (0, 0, 0)),
            pl.BlockSpec((BF16_ROWS, s), lambda b, p: (0, 0)),
        ],
        out_specs=pair_fm,
        out_shape=jax.ShapeDtypeStruct((bsz, w, s), BF16),
        scratch_shapes=[pltpu.VMEM((HEADS_PER_STEP, s, LANES), BF16),
                        pltpu.VMEM((HEADS_PER_STEP, LANES, s), BF16),
                        pltpu.VMEM((HEADS_PER_STEP, HEAD_DIM + BF16_ROWS, s), BF16),
                        pltpu.VMEM((ATTN_RING, MOBA_BLOCK, MOBA_BLOCK), F32)],
        compiler_params=pltpu.CompilerParams(
            dimension_semantics=("arbitrary", "arbitrary"), vmem_limit_bytes=VMEM_LIMIT),
        name="moba_attention",
    )(qt, k, vt, oh, avg)


def _outproj_kernel(at_ref, zat_ref, yl_ref, x_ref, mod_ref, ga_ref, wo_ref, fg_ref, o_ref,
                    *, final_norm):
    d = x_ref.shape[2]
    ts = x_ref.shape[1]
    w = ATTN_WIDTH
    gate = mod_ref[0, :, 2 * d:3 * d]
    tn = (((0,), (0,)), ((), ()))
    subs = [slice(r, r + OUT_SUB) for r in range(0, ts, OUT_SUB)]
    ya, y = {}, {}

    def prep(i):
        at = at_ref[0, :, subs[i]].astype(F32)
        rs = lax.rsqrt(jnp.mean(at * at, axis=0, keepdims=True) + EPS)
        z = zat_ref[0, :, subs[i]].astype(F32)
        ya[i] = ((at * rs) * ga_ref[...] * (z * _sigmoid(z))).astype(BF16)

    def project(i):
        yy = lax.dot_general(ya[i], wo_ref[0:w, :], tn, preferred_element_type=F32)
        y[i] = yy + jnp.dot(yl_ref[0, subs[i], :], wo_ref[w:2 * w, :],
                            preferred_element_type=F32)

    def finish(i):
        xo = x_ref[0, subs[i], :] + gate * y[i]
        if final_norm:
            xo = (xo * lax.rsqrt(jnp.mean(xo * xo, axis=-1, keepdims=True) + EPS)) * fg_ref[...]
        o_ref[0, subs[i], :] = xo

    n = len(subs)
    prep(0)
    for i in range(n):
        if i + 1 < n:
            prep(i + 1)
        project(i)
        if i > 0:
            finish(i - 1)
    finish(n - 1)


def _outproj(at, zat, yl, x, mod3, gain_a, w_out, final_gain, ts, final_norm):
    bsz, s, d = x.shape
    w = ATTN_WIDTH
    return pl.pallas_call(
        functools.partial(_outproj_kernel, final_norm=final_norm),
        grid=(bsz, s // ts),
        in_specs=[
            pl.BlockSpec((1, w, ts), lambda b, i: (b, 0, i)),
            pl.BlockSpec((1, w, ts), lambda b, i: (b, 0, i)),
            pl.BlockSpec((1, ts, w), lambda b, i: (b, i, 0)),
            pl.BlockSpec((1, ts, d), lambda b, i: (b, i, 0)),
            pl.BlockSpec((1, 1, 3 * d), lambda b, i: (b, 0, 0)),
            pl.BlockSpec((w, 1), lambda b, i: (0, 0)),
            pl.BlockSpec((2 * w, d), lambda b, i: (0, 0)),
            pl.BlockSpec((1, d), lambda b, i: (0, 0)),
        ],
        out_specs=pl.BlockSpec((1, ts, d), lambda b, i: (b, i, 0)),
        out_shape=jax.ShapeDtypeStruct((bsz, s, d), F32),
        compiler_params=pltpu.CompilerParams(
            dimension_semantics=("arbitrary", "arbitrary"), vmem_limit_bytes=VMEM_LIMIT),
        name="outproj",
    )(at, zat, yl, x, mod3, gain_a, w_out, final_gain)


def _rope_tables(s):
    pos = jnp.arange(s, dtype=F32)
    inv_freq = ROPE_THETA ** (-jnp.arange(0, HEAD_DIM, 2, dtype=F32) / HEAD_DIM)
    ang = pos[:, None] * inv_freq[None, :]
    cos, sin = jnp.cos(ang), jnp.sin(ang)
    cosn = jnp.tile(cos, (1, LANES // HALF_DIM))
    sinn = jnp.tile(jnp.concatenate([-sin, sin], axis=1), (1, LANES // HEAD_DIM))
    cost = jnp.concatenate([cos.T, cos.T], axis=0)
    sint = jnp.concatenate([-sin.T, sin.T], axis=0)
    return cosn, sinn, cost, sint


def _block_diag(wg):
    g, n, _ = wg.shape
    eye = jnp.eye(g, dtype=wg.dtype)
    return jnp.einsum("gij,gh->gihj", wg, eye).reshape(g * n, g * n)


def kernel(x, c, w_mod, b_mod, norm_gain, w_in, conv_w, conv_b, w_rgate, b_rgate,
           w_igate, b_igate, lru_lambda, attn_out_gain, lru_out_gain, w_out, final_gain):
    bsz, s, d = x.shape
    depth = w_in.shape[0]
    w = ATTN_WIDTH
    assert s % OUT_TILE == 0 and d % LANES == 0 and w_in.shape[2] == 6 * w
    cosn, sinn, cost, sint = _rope_tables(s)
    for l in range(depth):
        mod3 = _modulation(c, w_mod[l], b_mod[l]).reshape(bsz, 1, 3 * d)
        wb = w_in[l].astype(BF16)
        w_t = jnp.concatenate([wb[:, 0:w], wb[:, 2 * w:4 * w]], axis=1).T
        qt, vt, zat, k, yl = _inproj_lru(
            x, mod3, norm_gain[l].reshape(1, d), wb, w_t, cosn, sinn, cost, sint,
            conv_w[l], conv_b[l],
            _block_diag(w_rgate[l].astype(BF16)), _block_diag(w_igate[l].astype(BF16)),
            b_rgate[l].reshape(-1), b_igate[l].reshape(-1), lru_lambda[l], lru_out_gain[l],
            IN_TILE)
        at = _attention(qt, k, vt)
        x = _outproj(at, zat, yl, x, mod3, attn_out_gain[l].reshape(w, 1),
                     w_out[l].astype(BF16), final_gain.reshape(1, d), OUT_TILE,
                     final_norm=(l == depth - 1))
    return x
```

```python
import functools

import jax
import jax.numpy as jnp
from jax import lax
from jax.experimental import pallas as pl
from jax.experimental.pallas import tpu as pltpu

ATTN_HEADS = 8
HEAD_DIM = 64
HALF_DIM = HEAD_DIM // 2
ATTN_WIDTH = ATTN_HEADS * HEAD_DIM
LRU_BLOCKS = 8
MOBA_BLOCK = 256
MOBA_TOPK = 3
CONV_WIDTH = 4
LRU_C = 8.0
ROPE_THETA = 10000.0
EPS = 1e-6

LANES = 128
SUBLANES = 8
BF16_ROWS = 16
MASK_VALUE = -1e30
LOG2_E = 1.4426950408889634
VMEM_LIMIT = 56 * 1024 * 1024

F32 = jnp.float32
BF16 = jnp.bfloat16


def _sigmoid(z):
    return 1.0 / (1.0 + jnp.exp(-z))


def _split3(a):
    hi = a.astype(BF16)
    r1 = a - hi.astype(F32)
    mid = r1.astype(BF16)
    lo = (r1 - mid.astype(F32)).astype(BF16)
    return hi, mid, lo


def _mod_kernel(c_ref, w_ref, b_ref, o_ref):
    c = c_ref[...]
    act = c * _sigmoid(c)
    a_hi, a_mid, _ = _split3(act)
    w_hi, w_mid, _ = _split3(w_ref[...])
    dot = functools.partial(jnp.dot, preferred_element_type=F32)
    acc = dot(a_hi, w_hi) + dot(a_hi, w_mid) + dot(a_mid, w_hi)
    o_ref[...] = acc + b_ref[...]


def _modulation(c, w_mod, b_mod):
    bsz, d = c.shape
    n = w_mod.shape[1]
    tn = 512
    return pl.pallas_call(
        _mod_kernel,
        grid=(n // tn,),
        in_specs=[
            pl.BlockSpec((bsz, d), lambda j: (0, 0)),
            pl.BlockSpec((d, tn), lambda j: (0, j)),
            pl.BlockSpec((1, tn), lambda j: (0, j)),
        ],
        out_specs=pl.BlockSpec((bsz, tn), lambda j: (0, j)),
        out_shape=jax.ShapeDtypeStruct((bsz, n), F32),
        compiler_params=pltpu.CompilerParams(
            dimension_semantics=("arbitrary",), vmem_limit_bytes=VMEM_LIMIT),
        name="modulation",
    )(c, w_mod, b_mod.reshape(1, n))


IN_TILE = 1024
OUT_TILE = 2048
OUT_SUB = 256
K_COL = 1 * ATTN_WIDTH
XL_COL = 4 * ATTN_WIDTH
ZL_COL = 5 * ATTN_WIDTH
LRU_CHUNK = 128
PROJ_PIECE = 256
LRU_QUAD = 256


def _inproj_lru_kernel(x_ref, mod_ref, gain_ref, wn_ref, wt_ref, cosn_ref, sinn_ref,
                       cost_ref, sint_ref, cw_ref, cb_ref, wr_ref, wi_ref, br_ref, bi_ref,
                       lam_ref, lg_ref,
                       qt_ref, vt_ref, zat_ref, k_ref, yl_ref,
                       ext_buf, zl_buf, a_buf, u_buf, hc_ref):
    d = x_ref.shape[2]
    ts = x_ref.shape[1]
    w = ATTN_WIDTH
    pad = SUBLANES
    z0 = jnp.minimum(pl.program_id(1), 0)
    ext_ref, zl_ref, a_ref, u_ref = ext_buf.at[z0], zl_buf.at[z0], a_buf.at[z0], u_buf.at[z0]

    @pl.when(pl.program_id(1) == 0)
    def _():
        ext_ref[0:pad, :] = jnp.zeros((pad, w), F32)
        hc_ref[...] = jnp.zeros(hc_ref.shape, F32)

    x = x_ref[0]
    shift = mod_ref[0, :, 0:d]
    scale = mod_ref[0, :, d:2 * d]
    rs = lax.rsqrt(jnp.mean(x * x, axis=-1, keepdims=True) + EPS)
    h = (x * rs) * (gain_ref[...] * (1.0 + scale)) + shift
    hb = h.astype(BF16)

    ext_ref[pad:pad + ts, :] = jnp.dot(hb, wn_ref[:, XL_COL:XL_COL + w],
                                       preferred_element_type=F32)
    zl_ref[...] = jnp.dot(hb, wn_ref[:, ZL_COL:ZL_COL + w], preferred_element_type=F32)

    nt = (((1,), (1,)), ((), ()))
    half = PROJ_PIECE
    n_piece = w // PROJ_PIECE

    def k_piece(p):
        kk = jnp.dot(hb, wn_ref[:, K_COL + p * half:K_COL + (p + 1) * half],
                     preferred_element_type=F32)
        cosn = cosn_ref[...]
        sinn = sinn_ref[...]
        lane = lax.broadcasted_iota(jnp.int32, cosn.shape, 1)
        first_half = (lane % HEAD_DIM) < HALF_DIM
        for g in range(half // LANES):
            kg = kk[:, g * LANES:(g + 1) * LANES]
            swapped = jnp.where(first_half,
                                pltpu.roll(kg, LANES - HALF_DIM, axis=1),
                                pltpu.roll(kg, HALF_DIM, axis=1))
            lo = p * half + g * LANES
            k_ref[0, :, lo:lo + LANES] = (kg * cosn + swapped * sinn).astype(BF16)

    def q_piece(p):
        qt = lax.dot_general(wt_ref[p * half:(p + 1) * half, :], hb, nt,
                             preferred_element_type=F32)
        cost = cost_ref[...]
        sint = sint_ref[...]
        qscale = HEAD_DIM ** -0.5 * LOG2_E
        for hd in range(half // HEAD_DIM):
            t = qt[hd * HEAD_DIM:(hd + 1) * HEAD_DIM]
            swapped = jnp.concatenate([t[HALF_DIM:], t[:HALF_DIM]], axis=0)
            lo = p * half + hd * HEAD_DIM
            qt_ref[0, lo:lo + HEAD_DIM, :] = ((t * cost + swapped * sint) * qscale).astype(BF16)

    def fm_piece(out_ref, base, p):
        lo = p * half
        out_ref[0, lo:lo + half, :] = lax.dot_general(
            wt_ref[base + lo:base + lo + half, :], hb, nt,
            preferred_element_type=F32).astype(BF16)

    lam = lam_ref[...]
    neg_lam = -lam
    softplus = jnp.maximum(neg_lam, 0.0) + jnp.log1p(jnp.exp(-jnp.abs(neg_lam)))
    decay = (-LRU_C * LOG2_E) * softplus
    state = {"h": hc_ref[...]}

    def quad_dot(xb, w_ref):
        return jnp.concatenate(
            [jnp.dot(xb[:, q:q + LRU_QUAD], w_ref[q:q + LRU_QUAD, q:q + LRU_QUAD],
                     preferred_element_type=F32) for q in range(0, w, LRU_QUAD)], axis=1)

    def lru_gates(c):
        base = pad + c * LRU_CHUNK
        xc = cb_ref[...] + cw_ref[CONV_WIDTH - 1:CONV_WIDTH, :] * ext_ref[base:base + LRU_CHUNK, :]
        for tap in range(1, CONV_WIDTH):
            xc = xc + (cw_ref[CONV_WIDTH - 1 - tap:CONV_WIDTH - tap, :] *
                       ext_ref[base - tap:base - tap + LRU_CHUNK, :])
        xcb = xc.astype(BF16)
        r = _sigmoid(quad_dot(xcb, wr_ref) + br_ref[...])
        ig = _sigmoid(quad_dot(xcb, wi_ref) + bi_ref[...])
        a = jnp.exp2(decay * r)
        y = 1.0 - a * a
        u = jnp.where(y > 0.0, y * lax.rsqrt(y), 0.0) * ig * xc
        grp = (LRU_CHUNK // SUBLANES, SUBLANES, w)
        a = a.reshape(grp)
        u = u.reshape(grp)
        rowg = lax.broadcasted_iota(jnp.int32, grp, 1)
        for sh in (1, 2, 4):
            ok = rowg >= sh
            a_prev = pltpu.roll(a, sh, axis=1)
            u_prev = pltpu.roll(u, sh, axis=1)
            u = jnp.where(ok, a * u_prev + u, u)
            a = jnp.where(ok, a * a_prev, a)
        a_ref[c * LRU_CHUNK:(c + 1) * LRU_CHUNK, :] = a.reshape(LRU_CHUNK, w)
        u_ref[c * LRU_CHUNK:(c + 1) * LRU_CHUNK, :] = u.reshape(LRU_CHUNK, w)

    def lru_carry(c):
        hprev = state["h"]
        for g in range(LRU_CHUNK // SUBLANES):
            r0 = c * LRU_CHUNK + g * SUBLANES
            hcur = u_ref[r0:r0 + SUBLANES, :] + a_ref[r0:r0 + SUBLANES, :] * hprev
            u_ref[r0:r0 + SUBLANES, :] = hcur
            hprev = hcur[SUBLANES - 1:SUBLANES, :]
        state["h"] = hprev

    def lru_finish(c):
        rows = slice(c * LRU_CHUNK, (c + 1) * LRU_CHUNK)
        rec = u_ref[rows, :]
        rsq = lax.rsqrt(jnp.mean(rec * rec, axis=-1, keepdims=True) + EPS)
        z = zl_ref[rows, :]
        yl_ref[0, rows, :] = ((rec * rsq) * lg_ref[...] * (z * _sigmoid(z))).astype(BF16)

    mxu_pieces = ([functools.partial(k_piece, p) for p in range(n_piece)] +
                  [functools.partial(q_piece, p) for p in range(n_piece)] +
                  [functools.partial(fm_piece, vt_ref, w, p) for p in range(n_piece)] +
                  [functools.partial(fm_piece, zat_ref, 2 * w, p) for p in range(n_piece)])
    lru_pieces = []
    for c in range(ts // LRU_CHUNK):
        lru_pieces += [functools.partial(lru_gates, c), functools.partial(lru_carry, c),
                       functools.partial(lru_finish, c)]
    done = 0
    for i, piece in enumerate(mxu_pieces):
        piece()
        upto = (i + 1) * len(lru_pieces) // len(mxu_pieces)
        for lp in lru_pieces[done:upto]:
            lp()
        done = upto

    hc_ref[...] = state["h"]
    ext_ref[0:pad, :] = ext_ref[ts:ts + pad, :]


def _inproj_lru(x, mod3, gain, w_nat, w_t, cosn, sinn, cost, sint,
                conv_w, conv_b, wr_bd, wi_bd, b_r, b_i, lam, lru_gain, ts):
    bsz, s, d = x.shape
    w = ATTN_WIDTH
    fm = jax.ShapeDtypeStruct((bsz, w, s), BF16)
    sm = jax.ShapeDtypeStruct((bsz, s, w), BF16)
    fm_spec = pl.BlockSpec((1, w, ts), lambda b, i: (b, 0, i))
    sm_spec = pl.BlockSpec((1, ts, w), lambda b, i: (b, i, 0))
    vec_spec = pl.BlockSpec((1, w), lambda b, i: (0, 0))
    mat_spec = pl.BlockSpec((w, w), lambda b, i: (0, 0))
    row = lambda a: a.reshape(1, w)
    return pl.pallas_call(
        _inproj_lru_kernel,
        grid=(bsz, s // ts),
        in_specs=[
            pl.BlockSpec((1, ts, d), lambda b, i: (b, i, 0)),
            pl.BlockSpec((1, 1, 3 * d), lambda b, i: (b, 0, 0)),
            pl.BlockSpec((1, d), lambda b, i: (0, 0)),
            pl.BlockSpec((d, 6 * w), lambda b, i: (0, 0)),
            pl.BlockSpec((3 * w, d), lambda b, i: (0, 0)),
            pl.BlockSpec((ts, LANES), lambda b, i: (i, 0)),
            pl.BlockSpec((ts, LANES), lambda b, i: (i, 0)),
            pl.BlockSpec((HEAD_DIM, ts), lambda b, i: (0, i)),
            pl.BlockSpec((HEAD_DIM, ts), lambda b, i: (0, i)),
            pl.BlockSpec((CONV_WIDTH, w), lambda b, i: (0, 0)), vec_spec,
            mat_spec, mat_spec, vec_spec, vec_spec, vec_spec, vec_spec,
        ],
        out_specs=[fm_spec, fm_spec, fm_spec, sm_spec, sm_spec],
        out_shape=[fm, fm, fm, sm, sm],
        scratch_shapes=[pltpu.VMEM((1, ts + SUBLANES, w), F32),
                        pltpu.VMEM((1, ts, w), F32),
                        pltpu.VMEM((1, ts, w), F32),
                        pltpu.VMEM((1, ts, w), F32),
                        pltpu.VMEM((1, w), F32)],
        compiler_params=pltpu.CompilerParams(
            dimension_semantics=("arbitrary", "arbitrary"), vmem_limit_bytes=VMEM_LIMIT),
        name="inproj_rglru",
    )(x, mod3, gain, w_nat, w_t, cosn, sinn, cost, sint,
      conv_w, row(conv_b), wr_bd, wi_bd, row(b_r), row(b_i), row(lam), row(lru_gain))


HEADS_PER_GROUP = LANES // HEAD_DIM
HEADS_PER_STEP = 4
BLOCK_SHIFT = MOBA_BLOCK.bit_length() - 1
assert 1 << BLOCK_SHIFT == MOBA_BLOCK
ATTN_LEAD = 5
ATTN_RING = 16


def _attn_kernel(qt_ref, k_ref, vt_ref, oh_ref, avg_ref, o_ref, ka_ref, qa_ref, va_ref, s_ref):
    s = qt_ref.shape[2]
    nblk = s // MOBA_BLOCK

    jrow = lax.broadcasted_iota(jnp.int32, (nblk, s), 0)
    qblk = lax.shift_right_logical(lax.broadcasted_iota(jnp.int32, (nblk, s), 1), BLOCK_SHIFT)
    cand = jrow < qblk
    qb = qblk[0:1, :]
    kr = lax.broadcasted_iota(jnp.int32, (MOBA_BLOCK, MOBA_BLOCK), 0)
    qc = lax.broadcasted_iota(jnp.int32, (MOBA_BLOCK, MOBA_BLOCK), 1)
    causal = kr <= qc

    for hh in range(HEADS_PER_STEP):
        grp, hp = divmod(hh, HEADS_PER_GROUP)
        grp_lanes = slice(grp * LANES, (grp + 1) * LANES)
        head_lo = hp * HEAD_DIM
        aux_lo = HEAD_DIM - head_lo
        rows_l = slice(head_lo, head_lo + HEAD_DIM)
        rows_h = slice(hh * HEAD_DIM, (hh + 1) * HEAD_DIM)

        if hp == 0:
            kmean = jnp.dot(avg_ref[...], k_ref[0, :, grp_lanes], preferred_element_type=F32)
            km3 = jnp.concatenate(_split3(kmean), axis=0)

        ka_ref[hh] = oh_ref[hp]
        ka_ref[hh, :, rows_l] = k_ref[0, :, grp * LANES + head_lo:grp * LANES + head_lo + HEAD_DIM]

        va_ref[hh, 0:HEAD_DIM, :] = vt_ref[0, rows_h, :]
        va_ref[hh, HEAD_DIM:, :] = jnp.ones((BF16_ROWS, s), BF16)

        qa_ref[hh] = jnp.zeros(qa_ref.shape[1:], BF16)
        qa_ref[hh, rows_l, :] = qt_ref[0, rows_h, :]
        g3 = jnp.dot(km3, qa_ref[hh], preferred_element_type=F32)
        gate = (g3[0:BF16_ROWS] + g3[BF16_ROWS:2 * BF16_ROWS]) + g3[2 * BF16_ROWS:]
        gate = gate[0:nblk]

        bias_rows = []
        for j in range(nblk):
            gj = gate[j:j + 1, :]
            beats = cand & ((gate > gj) | ((gate == gj) & (jrow < j)))
            rank = jnp.sum(beats.astype(F32), axis=0, keepdims=True)
            keep = ((qb > j) & (rank < MOBA_TOPK)) | (qb == j)
            bias_rows.append(jnp.where(keep, 0.0, MASK_VALUE))
        bias = jnp.concatenate(bias_rows + [jnp.zeros((BF16_ROWS - nblk, s), F32)], axis=0)
        qa_ref[hh, aux_lo:aux_lo + BF16_ROWS, :] = bias.astype(BF16)

    items = [(i, hh) for hh in range(HEADS_PER_STEP)
             for i in (range(nblk) if hh % 2 == 0 else reversed(range(nblk)))]
    piece_list = [(n, j) for n, (i, _) in enumerate(items) for j in range(i + 1)]
    item_end = {}
    for t, (n, _) in enumerate(piece_list):
        item_end[n] = t + 1
    state = [dict() for _ in items]
    ring = s_ref.shape[0]
    slot0 = jnp.minimum(pl.program_id(1), 0)

    def blk_rows(j):
        return slice(j * MOBA_BLOCK, (j + 1) * MOBA_BLOCK)

    def score_piece(t):
        n, j = piece_list[t]
        i, hh = items[n]
        sc = jnp.dot(ka_ref[hh, blk_rows(j), :], qa_ref[hh, :, blk_rows(i)],
                     preferred_element_type=F32)
        if j == i:
            sc = jnp.where(causal, sc, MASK_VALUE)
        s_ref[slot0 + t % ring] = sc
        cm = jnp.max(sc, axis=0, keepdims=True)
        st = state[n]
        st["m"] = cm if j == 0 else jnp.maximum(st["m"], cm)

    def softmax_piece(t):
        n, j = piece_list[t]
        i, hh = items[n]
        rows_h = slice(hh * HEAD_DIM, (hh + 1) * HEAD_DIM)
        st = state[n]
        p = jnp.exp2(s_ref[slot0 + t % ring] - st["m"])
        pv = jnp.dot(va_ref[hh, :, blk_rows(j)], p.astype(BF16),
                     preferred_element_type=F32)
        st["acc"] = pv if j == 0 else st["acc"] + pv
        if j == i:
            acc = st["acc"]
            o_ref[0, rows_h, blk_rows(i)] = (
                acc[0:HEAD_DIM] / acc[HEAD_DIM:HEAD_DIM + 1]).astype(BF16)

    total = len(piece_list)
    scored = softmaxed = 0
    while softmaxed < total:
        if scored < total:
            assert scored - softmaxed < ring, "score ring too small for the softmax lag"
            score_piece(scored)
            scored += 1
        ready = min(total, item_end[piece_list[softmaxed][0]] + ATTN_LEAD)
        if scored >= ready:
            softmax_piece(softmaxed)
            softmaxed += 1


def _attention(qt, k, vt):
    bsz, w, s = qt.shape
    blk = jnp.arange(s, dtype=jnp.int32) // MOBA_BLOCK
    lane = jnp.arange(LANES, dtype=jnp.int32)
    oh = jnp.stack([(lane[None, :] == (HEAD_DIM - hp * HEAD_DIM) + blk[:, None])
                    for hp in range(HEADS_PER_GROUP)]).astype(BF16)
    avg = jnp.where(jnp.arange(BF16_ROWS, dtype=jnp.int32)[:, None] == blk[None, :],
                    1.0 / MOBA_BLOCK, 0.0).astype(BF16)
    step_w = HEADS_PER_STEP * HEAD_DIM
    pair_fm = pl.BlockSpec((1, step_w, s), lambda b, p: (b, p, 0))
    return pl.pallas_call(
        _attn_kernel,
        grid=(bsz, ATTN_HEADS // HEADS_PER_STEP),
        in_specs=[
            pair_fm,
            pl.BlockSpec((1, s, step_w), lambda b, p: (b, 0, p)),
            pair_fm,
            pl.BlockSpec((HEADS_PER_GROUP, s, LANES), lambda b, p: (0, 0, 0)),
            pl.BlockSpec((BF16_ROWS, s), lambda b, p: (0, 0)),
        ],
        out_specs=pair_fm,
        out_shape=jax.ShapeDtypeStruct((bsz, w, s), BF16),
        scratch_shapes=[pltpu.VMEM((HEADS_PER_STEP, s, LANES), BF16),
                        pltpu.VMEM((HEADS_PER_STEP, LANES, s), BF16),
                        pltpu.VMEM((HEADS_PER_STEP, HEAD_DIM + BF16_ROWS, s), BF16),
                        pltpu.VMEM((ATTN_RING, MOBA_BLOCK, MOBA_BLOCK), F32)],
        compiler_params=pltpu.CompilerParams(
            dimension_semantics=("arbitrary", "arbitrary"), vmem_limit_bytes=VMEM_LIMIT),
        name="moba_attention",
    )(qt, k, vt, oh, avg)


def _outproj_kernel(at_ref, zat_ref, yl_ref, x_ref, mod_ref, ga_ref, wo_ref, fg_ref, o_ref,
                    *, final_norm):
    d = x_ref.shape[2]
    ts = x_ref.shape[1]
    w = ATTN_WIDTH
    gate = mod_ref[0, :, 2 * d:3 * d]
    tn = (((0,), (0,)), ((), ()))
    subs = [slice(r, r + OUT_SUB) for r in range(0, ts, OUT_SUB)]
    ya, y = {}, {}

    def prep(i):
        at = at_ref[0, :, subs[i]].astype(F32)
        rs = lax.rsqrt(jnp.mean(at * at, axis=0, keepdims=True) + EPS)
        z = zat_ref[0, :, subs[i]].astype(F32)
        ya[i] = ((at * rs) * ga_ref[...] * (z * _sigmoid(z))).astype(BF16)

    def project(i):
        yy = lax.dot_general(ya[i], wo_ref[0:w, :], tn, preferred_element_type=F32)
        y[i] = yy + jnp.dot(yl_ref[0, subs[i], :], wo_ref[w:2 * w, :],
                            preferred_element_type=F32)

    def finish(i):
        xo = x_ref[0, subs[i], :] + gate * y[i]
        if final_norm:
            xo = (xo * lax.rsqrt(jnp.mean(xo * xo, axis=-1, keepdims=True) + EPS)) * fg_ref[...]
        o_ref[0, subs[i], :] = xo

    n = len(subs)
    prep(0)
    for i in range(n):
        if i + 1 < n:
            prep(i + 1)
        project(i)
        if i > 0:
            finish(i - 1)
    finish(n - 1)


def _outproj(at, zat, yl, x, mod3, gain_a, w_out, final_gain, ts, final_norm):
    bsz, s, d = x.shape
    w = ATTN_WIDTH
    return pl.pallas_call(
        functools.partial(_outproj_kernel, final_norm=final_norm),
        grid=(bsz, s // ts),
        in_specs=[
            pl.BlockSpec((1, w, ts), lambda b, i: (b, 0, i)),
            pl.BlockSpec((1, w, ts), lambda b, i: (b, 0, i)),
            pl.BlockSpec((1, ts, w), lambda b, i: (b, i, 0)),
            pl.BlockSpec((1, ts, d), lambda b, i: (b, i, 0)),
            pl.BlockSpec((1, 1, 3 * d), lambda b, i: (b, 0, 0)),
            pl.BlockSpec((w, 1), lambda b, i: (0, 0)),
            pl.BlockSpec((2 * w, d), lambda b, i: (0, 0)),
            pl.BlockSpec((1, d), lambda b, i: (0, 0)),
        ],
        out_specs=pl.BlockSpec((1, ts, d), lambda b, i: (b, i, 0)),
        out_shape=jax.ShapeDtypeStruct((bsz, s, d), F32),
        compiler_params=pltpu.CompilerParams(
            dimension_semantics=("arbitrary", "arbitrary"), vmem_limit_bytes=VMEM_LIMIT),
        name="outproj",
    )(at, zat, yl, x, mod3, gain_a, w_out, final_gain)


def _rope_tables(s):
    pos = jnp.arange(s, dtype=F32)
    inv_freq = ROPE_THETA ** (-jnp.arange(0, HEAD_DIM, 2, dtype=F32) / HEAD_DIM)
    ang = pos[:, None] * inv_freq[None, :]
    cos, sin = jnp.cos(ang), jnp.sin(ang)
    cosn = jnp.tile(cos, (1, LANES // HALF_DIM))
    sinn = jnp.tile(jnp.concatenate([-sin, sin], axis=1), (1, LANES // HEAD_DIM))
    cost = jnp.concatenate([cos.T, cos.T], axis=0)
    sint = jnp.concatenate([-sin.T, sin.T], axis=0)
    return cosn, sinn, cost, sint


def _block_diag(wg):
    g, n, _ = wg.shape
    eye = jnp.eye(g, dtype=wg.dtype)
    return jnp.einsum("gij,gh->gihj", wg, eye).reshape(g * n, g * n)


def kernel(x, c, w_mod, b_mod, norm_gain, w_in, conv_w, conv_b, w_rgate, b_rgate,
           w_igate, b_igate, lru_lambda, attn_out_gain, lru_out_gain, w_out, final_gain):
    bsz, s, d = x.shape
    depth = w_in.shape[0]
    w = ATTN_WIDTH
    assert s % OUT_TILE == 0 and d % LANES == 0 and w_in.shape[2] == 6 * w
    cosn, sinn, cost, sint = _rope_tables(s)
    for l in range(depth):
        mod3 = _modulation(c, w_mod[l], b_mod[l]).reshape(bsz, 1, 3 * d)
        wb = w_in[l].astype(BF16)
        w_t = jnp.concatenate([wb[:, 0:w], wb[:, 2 * w:4 * w]], axis=1).T
        qt, vt, zat, k, yl = _inproj_lru(
            x, mod3, norm_gain[l].reshape(1, d), wb, w_t, cosn, sinn, cost, sint,
            conv_w[l], conv_b[l],
            _block_diag(w_rgate[l].astype(BF16)), _block_diag(w_igate[l].astype(BF16)),
            b_rgate[l].reshape(-1), b_igate[l].reshape(-1), lru_lambda[l], lru_out_gain[l],
            IN_TILE)
        at = _attention(qt, k, vt)
        x = _outproj(at, zat, yl, x, mod3, attn_out_gain[l].reshape(w, 1),
                     w_out[l].astype(BF16), final_gain.reshape(1, d), OUT_TILE,
                     final_norm=(l == depth - 1))
    return x
```

```python
import functools

import jax
import jax.numpy as jnp
import numpy as np
from jax import lax
from jax.experimental import pallas as pl
from jax.experimental.pallas import tpu as pltpu

ATTN_HEADS = 8
HEAD_DIM = 64
HALF_DIM = HEAD_DIM // 2
ATTN_WIDTH = ATTN_HEADS * HEAD_DIM
LRU_BLOCKS = 8
MOBA_BLOCK = 256
MOBA_TOPK = 3
CONV_WIDTH = 4
LRU_C = 8.0
ROPE_THETA = 10000.0
EPS = 1e-6

LANES = 128
SUBLANES = 8
BF16_ROWS = 16
MASK_VALUE = -1e30
LOG2_E = 1.4426950408889634
VMEM_LIMIT = 56 * 1024 * 1024

F32 = jnp.float32
BF16 = jnp.bfloat16


def _sigmoid(z):
    return 0.5 * jnp.tanh(0.5 * z) + 0.5


def _split3(a):
    hi = a.astype(BF16)
    r1 = a - hi.astype(F32)
    mid = r1.astype(BF16)
    lo = (r1 - mid.astype(F32)).astype(BF16)
    return hi, mid, lo


def _mod_kernel(c_ref, w_ref, b_ref, o_ref):
    c = c_ref[...]
    act = c * _sigmoid(c)
    a_hi, a_mid, _ = _split3(act)
    w_hi, w_mid, _ = _split3(w_ref[...])
    dot = functools.partial(jnp.dot, preferred_element_type=F32)
    acc = dot(a_hi, w_hi) + dot(a_hi, w_mid) + dot(a_mid, w_hi)
    o_ref[...] = acc + b_ref[...]


def _modulation(c, w_mod, b_mod):
    bsz, d = c.shape
    n = w_mod.shape[1]
    tn = 512
    return pl.pallas_call(
        _mod_kernel,
        grid=(n // tn,),
        in_specs=[
            pl.BlockSpec((bsz, d), lambda j: (0, 0)),
            pl.BlockSpec((d, tn), lambda j: (0, j)),
            pl.BlockSpec((1, tn), lambda j: (0, j)),
        ],
        out_specs=pl.BlockSpec((bsz, tn), lambda j: (0, j)),
        out_shape=jax.ShapeDtypeStruct((bsz, n), F32),
        compiler_params=pltpu.CompilerParams(
            dimension_semantics=("arbitrary",), vmem_limit_bytes=VMEM_LIMIT),
        name="modulation",
    )(c, w_mod, b_mod.reshape(1, n))


IN_TILE = 1024
OUT_TILE = 2048
OUT_SUB = 256
K_COL = 1 * ATTN_WIDTH
XL_COL = 4 * ATTN_WIDTH
ZL_COL = 5 * ATTN_WIDTH
LRU_CHUNK = 128
PROJ_PIECE = 256
LRU_QUAD = 256


def _inproj_lru_kernel(x_ref, mod_ref, gain_ref, wn_ref, wt_ref, cosn_ref, sinn_ref,
                       cost_ref, sint_ref, cw_ref, cb_ref, wr_ref, wi_ref, br_ref, bi_ref,
                       lam_ref, lg_ref,
                       qt_ref, vt_ref, zat_ref, k_ref, yl_ref,
                       ext_buf, zl_buf, a_buf, u_buf, hc_ref):
    d = x_ref.shape[2]
    ts = x_ref.shape[1]
    w = ATTN_WIDTH
    pad = SUBLANES
    z0 = jnp.minimum(pl.program_id(1), 0)
    ext_ref, zl_ref, a_ref, u_ref = ext_buf.at[z0], zl_buf.at[z0], a_buf.at[z0], u_buf.at[z0]

    @pl.when(pl.program_id(1) == 0)
    def _():
        ext_ref[0:pad, :] = jnp.zeros((pad, w), F32)
        hc_ref[...] = jnp.zeros(hc_ref.shape, F32)

    x = x_ref[0]
    shift = mod_ref[0, :, 0:d]
    scale = mod_ref[0, :, d:2 * d]
    rs = lax.rsqrt(jnp.mean(x * x, axis=-1, keepdims=True) + EPS)
    h = (x * rs) * (gain_ref[...] * (1.0 + scale)) + shift
    hb = h.astype(BF16)

    ext_ref[pad:pad + ts, :] = jnp.dot(hb, wn_ref[:, XL_COL:XL_COL + w],
                                       preferred_element_type=F32)
    zl_ref[...] = jnp.dot(hb, wn_ref[:, ZL_COL:ZL_COL + w], preferred_element_type=F32)

    nt = (((1,), (1,)), ((), ()))
    half = PROJ_PIECE
    n_piece = w // PROJ_PIECE

    def k_piece(p):
        kk = jnp.dot(hb, wn_ref[:, K_COL + p * half:K_COL + (p + 1) * half],
                     preferred_element_type=F32)
        cosn = cosn_ref[...]
        sinn = sinn_ref[...]
        lane = lax.broadcasted_iota(jnp.int32, cosn.shape, 1)
        first_half = (lane % HEAD_DIM) < HALF_DIM
        for g in range(half // LANES):
            kg = kk[:, g * LANES:(g + 1) * LANES]
            swapped = jnp.where(first_half,
                                pltpu.roll(kg, LANES - HALF_DIM, axis=1),
                                pltpu.roll(kg, HALF_DIM, axis=1))
            lo = p * half + g * LANES
            k_ref[0, :, lo:lo + LANES] = (kg * cosn + swapped * sinn).astype(BF16)

    def q_piece(p):
        qt = lax.dot_general(wt_ref[p * half:(p + 1) * half, :], hb, nt,
                             preferred_element_type=F32)
        cost = cost_ref[...]
        sint = sint_ref[...]
        qscale = HEAD_DIM ** -0.5 * LOG2_E
        for hd in range(half // HEAD_DIM):
            t = qt[hd * HEAD_DIM:(hd + 1) * HEAD_DIM]
            swapped = jnp.concatenate([t[HALF_DIM:], t[:HALF_DIM]], axis=0)
            lo = p * half + hd * HEAD_DIM
            qt_ref[0, lo:lo + HEAD_DIM, :] = ((t * cost + swapped * sint) * qscale).astype(BF16)

    def fm_piece(out_ref, base, p):
        lo = p * half
        out_ref[0, lo:lo + half, :] = lax.dot_general(
            wt_ref[base + lo:base + lo + half, :], hb, nt,
            preferred_element_type=F32).astype(BF16)

    lam = lam_ref[...]
    neg_lam = -lam
    softplus = jnp.maximum(neg_lam, 0.0) + jnp.log1p(jnp.exp(-jnp.abs(neg_lam)))
    decay = (-LRU_C * LOG2_E) * softplus
    state = {"h": hc_ref[...]}

    def quad_dot(xb, w_ref):
        return jnp.concatenate(
            [jnp.dot(xb[:, q:q + LRU_QUAD], w_ref[q:q + LRU_QUAD, q:q + LRU_QUAD],
                     preferred_element_type=F32) for q in range(0, w, LRU_QUAD)], axis=1)

    def lru_gates(c):
        base = pad + c * LRU_CHUNK
        xc = cb_ref[...] + cw_ref[CONV_WIDTH - 1:CONV_WIDTH, :] * ext_ref[base:base + LRU_CHUNK, :]
        for tap in range(1, CONV_WIDTH):
            xc = xc + (cw_ref[CONV_WIDTH - 1 - tap:CONV_WIDTH - tap, :] *
                       ext_ref[base - tap:base - tap + LRU_CHUNK, :])
        xcb = xc.astype(BF16)
        r = _sigmoid(quad_dot(xcb, wr_ref) + br_ref[...])
        ig = _sigmoid(quad_dot(xcb, wi_ref) + bi_ref[...])
        a = jnp.exp2(decay * r)
        y = 1.0 - a * a
        u = jnp.where(y > 0.0, y * lax.rsqrt(y), 0.0) * ig * xc
        grp = (LRU_CHUNK // SUBLANES, SUBLANES, w)
        a = a.reshape(grp)
        u = u.reshape(grp)
        rowg = lax.broadcasted_iota(jnp.int32, grp, 1)
        for sh in (1, 2, 4):
            ok = rowg >= sh
            a_prev = pltpu.roll(a, sh, axis=1)
            u_prev = pltpu.roll(u, sh, axis=1)
            u = jnp.where(ok, a * u_prev + u, u)
            a = jnp.where(ok, a * a_prev, a)
        a_ref[c * LRU_CHUNK:(c + 1) * LRU_CHUNK, :] = a.reshape(LRU_CHUNK, w)
        u_ref[c * LRU_CHUNK:(c + 1) * LRU_CHUNK, :] = u.reshape(LRU_CHUNK, w)

    def lru_carry(c):
        hprev = state["h"]
        for g in range(LRU_CHUNK // SUBLANES):
            r0 = c * LRU_CHUNK + g * SUBLANES
            hcur = u_ref[r0:r0 + SUBLANES, :] + a_ref[r0:r0 + SUBLANES, :] * hprev
            u_ref[r0:r0 + SUBLANES, :] = hcur
            hprev = hcur[SUBLANES - 1:SUBLANES, :]
        state["h"] = hprev

    def lru_finish(c):
        rows = slice(c * LRU_CHUNK, (c + 1) * LRU_CHUNK)
        rec = u_ref[rows, :]
        rsq = lax.rsqrt(jnp.mean(rec * rec, axis=-1, keepdims=True) + EPS)
        z = zl_ref[rows, :]
        yl_ref[0, rows, :] = ((rec * rsq) * lg_ref[...] * (z * _sigmoid(z))).astype(BF16)

    mxu_pieces = ([functools.partial(k_piece, p) for p in range(n_piece)] +
                  [functools.partial(q_piece, p) for p in range(n_piece)] +
                  [functools.partial(fm_piece, vt_ref, w, p) for p in range(n_piece)] +
                  [functools.partial(fm_piece, zat_ref, 2 * w, p) for p in range(n_piece)])
    lru_pieces = []
    for c in range(ts // LRU_CHUNK):
        lru_pieces += [functools.partial(lru_gates, c), functools.partial(lru_carry, c),
                       functools.partial(lru_finish, c)]
    done = 0
    for i, piece in enumerate(mxu_pieces):
        piece()
        upto = (i + 1) * len(lru_pieces) // len(mxu_pieces)
        for lp in lru_pieces[done:upto]:
            lp()
        done = upto

    hc_ref[...] = state["h"]
    ext_ref[0:pad, :] = ext_ref[ts:ts + pad, :]


def _inproj_lru(x, mod3, gain, w_nat, w_t, cosn, sinn, cost, sint,
                conv_w, conv_b, wr_bd, wi_bd, b_r, b_i, lam, lru_gain, ts):
    bsz, s, d = x.shape
    w = ATTN_WIDTH
    fm = jax.ShapeDtypeStruct((bsz, w, s), BF16)
    sm = jax.ShapeDtypeStruct((bsz, s, w), BF16)
    fm_spec = pl.BlockSpec((1, w, ts), lambda b, i: (b, 0, i))
    sm_spec = pl.BlockSpec((1, ts, w), lambda b, i: (b, i, 0))
    vec_spec = pl.BlockSpec((1, w), lambda b, i: (0, 0))
    mat_spec = pl.BlockSpec((w, w), lambda b, i: (0, 0))
    row = lambda a: a.reshape(1, w)
    return pl.pallas_call(
        _inproj_lru_kernel,
        grid=(bsz, s // ts),
        in_specs=[
            pl.BlockSpec((1, ts, d), lambda b, i: (b, i, 0)),
            pl.BlockSpec((1, 1, 3 * d), lambda b, i: (b, 0, 0)),
            pl.BlockSpec((1, d), lambda b, i: (0, 0)),
            pl.BlockSpec((d, 6 * w), lambda b, i: (0, 0)),
            pl.BlockSpec((3 * w, d), lambda b, i: (0, 0)),
            pl.BlockSpec((ts, LANES), lambda b, i: (i, 0)),
            pl.BlockSpec((ts, LANES), lambda b, i: (i, 0)),
            pl.BlockSpec((HEAD_DIM, ts), lambda b, i: (0, i)),
            pl.BlockSpec((HEAD_DIM, ts), lambda b, i: (0, i)),
            pl.BlockSpec((CONV_WIDTH, w), lambda b, i: (0, 0)), vec_spec,
            mat_spec, mat_spec, vec_spec, vec_spec, vec_spec, vec_spec,
        ],
        out_specs=[fm_spec, fm_spec, fm_spec, sm_spec, sm_spec],
        out_shape=[fm, fm, fm, sm, sm],
        scratch_shapes=[pltpu.VMEM((1, ts + SUBLANES, w), F32),
                        pltpu.VMEM((1, ts, w), F32),
                        pltpu.VMEM((1, ts, w), F32),
                        pltpu.VMEM((1, ts, w), F32),
                        pltpu.VMEM((1, w), F32)],
        compiler_params=pltpu.CompilerParams(
            dimension_semantics=("arbitrary", "arbitrary"), vmem_limit_bytes=VMEM_LIMIT),
        name="inproj_rglru",
    )(x, mod3, gain, w_nat, w_t, cosn, sinn, cost, sint,
      conv_w, row(conv_b), wr_bd, wi_bd, row(b_r), row(b_i), row(lam), row(lru_gain))


HEADS_PER_GROUP = LANES // HEAD_DIM
HEADS_PER_STEP = 4
BLOCK_SHIFT = MOBA_BLOCK.bit_length() - 1
assert 1 << BLOCK_SHIFT == MOBA_BLOCK
ATTN_LEAD = 5
ATTN_RING = 16


def _attn_kernel(qt_ref, k_ref, vt_ref, oh_ref, avg_ref, o_ref, ka_ref, qa_ref, va_ref, s_ref):
    s = qt_ref.shape[2]
    nblk = s // MOBA_BLOCK

    jrow = lax.broadcasted_iota(jnp.int32, (nblk, s), 0)
    qblk = lax.shift_right_logical(lax.broadcasted_iota(jnp.int32, (nblk, s), 1), BLOCK_SHIFT)
    cand = jrow < qblk
    qb = qblk[0:1, :]
    kr = lax.broadcasted_iota(jnp.int32, (MOBA_BLOCK, MOBA_BLOCK), 0)
    qc = lax.broadcasted_iota(jnp.int32, (MOBA_BLOCK, MOBA_BLOCK), 1)
    causal = kr <= qc

    for hh in range(HEADS_PER_STEP):
        grp, hp = divmod(hh, HEADS_PER_GROUP)
        grp_lanes = slice(grp * LANES, (grp + 1) * LANES)
        head_lo = hp * HEAD_DIM
        aux_lo = HEAD_DIM - head_lo
        rows_l = slice(head_lo, head_lo + HEAD_DIM)
        rows_h = slice(hh * HEAD_DIM, (hh + 1) * HEAD_DIM)

        if hp == 0:
            kmean = jnp.dot(avg_ref[...], k_ref[0, :, grp_lanes], preferred_element_type=F32)
            km3 = jnp.concatenate(_split3(kmean), axis=0)

        ka_ref[hh] = oh_ref[hp]
        ka_ref[hh, :, rows_l] = k_ref[0, :, grp * LANES + head_lo:grp * LANES + head_lo + HEAD_DIM]

        va_ref[hh, 0:HEAD_DIM, :] = vt_ref[0, rows_h, :]
        va_ref[hh, HEAD_DIM:, :] = jnp.ones((BF16_ROWS, s), BF16)

        qa_ref[hh] = jnp.zeros(qa_ref.shape[1:], BF16)
        qa_ref[hh, rows_l, :] = qt_ref[0, rows_h, :]
        g3 = jnp.dot(km3, qa_ref[hh], preferred_element_type=F32)
        gate = (g3[0:BF16_ROWS] + g3[BF16_ROWS:2 * BF16_ROWS]) + g3[2 * BF16_ROWS:]
        gate = gate[0:nblk]

        bias_rows = []
        for j in range(nblk):
            gj = gate[j:j + 1, :]
            beats = cand & ((gate > gj) | ((gate == gj) & (jrow < j)))
            rank = jnp.sum(beats.astype(F32), axis=0, keepdims=True)
            keep = ((qb > j) & (rank < MOBA_TOPK)) | (qb == j)
            bias_rows.append(jnp.where(keep, 0.0, MASK_VALUE))
        bias = jnp.concatenate(bias_rows + [jnp.zeros((BF16_ROWS - nblk, s), F32)], axis=0)
        qa_ref[hh, aux_lo:aux_lo + BF16_ROWS, :] = bias.astype(BF16)

    items = [(i, hh) for hh in range(HEADS_PER_STEP)
             for i in (range(nblk) if hh % 2 == 0 else reversed(range(nblk)))]
    piece_list = [(n, j) for n, (i, _) in enumerate(items) for j in range(i + 1)]
    item_end = {}
    for t, (n, _) in enumerate(piece_list):
        item_end[n] = t + 1
    state = [dict() for _ in items]
    ring = s_ref.shape[0]
    slot0 = jnp.minimum(pl.program_id(1), 0)

    def blk_rows(j):
        return slice(j * MOBA_BLOCK, (j + 1) * MOBA_BLOCK)

    def score_piece(t):
        n, j = piece_list[t]
        i, hh = items[n]
        sc = jnp.dot(ka_ref[hh, blk_rows(j), :], qa_ref[hh, :, blk_rows(i)],
                     preferred_element_type=F32)
        if j == i:
            sc = jnp.where(causal, sc, MASK_VALUE)
        s_ref[slot0 + t % ring] = sc
        cm = jnp.max(sc, axis=0, keepdims=True)
        st = state[n]
        st["m"] = cm if j == 0 else jnp.maximum(st["m"], cm)

    def softmax_piece(t):
        n, j = piece_list[t]
        i, hh = items[n]
        rows_h = slice(hh * HEAD_DIM, (hh + 1) * HEAD_DIM)
        st = state[n]
        p = jnp.exp2(s_ref[slot0 + t % ring] - st["m"])
        pv = jnp.dot(va_ref[hh, :, blk_rows(j)], p.astype(BF16),
                     preferred_element_type=F32)
        st["acc"] = pv if j == 0 else st["acc"] + pv
        if j == i:
            acc = st["acc"]
            o_ref[0, rows_h, blk_rows(i)] = (
                acc[0:HEAD_DIM] / acc[HEAD_DIM:HEAD_DIM + 1]).astype(BF16)

    total = len(piece_list)
    scored = softmaxed = 0
    while softmaxed < total:
        if scored < total:
            assert scored - softmaxed < ring, "score ring too small for the softmax lag"
            score_piece(scored)
            scored += 1
        ready = min(total, item_end[piece_list[softmaxed][0]] + ATTN_LEAD)
        if scored >= ready:
            softmax_piece(softmaxed)
            softmaxed += 1


def _attention(qt, k, vt):
    bsz, w, s = qt.shape
    blk = np.arange(s) // MOBA_BLOCK
    lane = np.arange(LANES)
    oh = jnp.asarray(np.stack([(lane[None, :] == (HEAD_DIM - hp * HEAD_DIM) + blk[:, None])
                               for hp in range(HEADS_PER_GROUP)]), dtype=BF16)
    avg = jnp.asarray(np.where(np.arange(BF16_ROWS)[:, None] == blk[None, :],
                               1.0 / MOBA_BLOCK, 0.0), dtype=BF16)
    step_w = HEADS_PER_STEP * HEAD_DIM
    pair_fm = pl.BlockSpec((1, step_w, s), lambda b, p: (b, p, 0))
    return pl.pallas_call(
        _attn_kernel,
        grid=(bsz, ATTN_HEADS // HEADS_PER_STEP),
        in_specs=[
            pair_fm,
            pl.BlockSpec((1, s, step_w), lambda b, p: (b, 0, p)),
            pair_fm,
            pl.BlockSpec((HEADS_PER_GROUP, s, LANES), lambda b, p: (0, 0, 0)),
            pl.BlockSpec((BF16_ROWS, s), lambda b, p: (0, 0)),
        ],
        out_specs=pair_fm,
        out_shape=jax.ShapeDtypeStruct((bsz, w, s), BF16),
        scratch_shapes=[pltpu.VMEM((HEADS_PER_STEP, s, LANES), BF16),
                        pltpu.VMEM((HEADS_PER_STEP, LANES, s), BF16),
                        pltpu.VMEM((HEADS_PER_STEP, HEAD_DIM + BF16_ROWS, s), BF16),
                        pltpu.VMEM((ATTN_RING, MOBA_BLOCK, MOBA_BLOCK), F32)],
        compiler_params=pltpu.CompilerParams(
            dimension_semantics=("arbitrary", "arbitrary"), vmem_limit_bytes=VMEM_LIMIT),
        name="moba_attention",
    )(qt, k, vt, oh, avg)


def _outproj_kernel(at_ref, zat_ref, yl_ref, x_ref, mod_ref, ga_ref, wo_ref, fg_ref, o_ref,
                    *, final_norm):
    d = x_ref.shape[2]
    ts = x_ref.shape[1]
    w = ATTN_WIDTH
    gate = mod_ref[0, :, 2 * d:3 * d]
    tn = (((0,), (0,)), ((), ()))
    subs = [slice(r, r + OUT_SUB) for r in range(0, ts, OUT_SUB)]
    ya, y = {}, {}

    def prep(i):
        at = at_ref[0, :, subs[i]].astype(F32)
        rs = lax.rsqrt(jnp.mean(at * at, axis=0, keepdims=True) + EPS)
        z = zat_ref[0, :, subs[i]].astype(F32)
        ya[i] = ((at * rs) * ga_ref[...] * (z * _sigmoid(z))).astype(BF16)

    def project(i):
        yy = lax.dot_general(ya[i], wo_ref[0:w, :], tn, preferred_element_type=F32)
        y[i] = yy + jnp.dot(yl_ref[0, subs[i], :], wo_ref[w:2 * w, :],
                            preferred_element_type=F32)

    def finish(i):
        xo = x_ref[0, subs[i], :] + gate * y[i]
        if final_norm:
            xo = (xo * lax.rsqrt(jnp.mean(xo * xo, axis=-1, keepdims=True) + EPS)) * fg_ref[...]
        o_ref[0, subs[i], :] = xo

    n = len(subs)
    prep(0)
    for i in range(n):
        if i + 1 < n:
            prep(i + 1)
        project(i)
        if i > 0:
            finish(i - 1)
    finish(n - 1)


def _outproj(at, zat, yl, x, mod3, gain_a, w_out, final_gain, ts, final_norm):
    bsz, s, d = x.shape
    w = ATTN_WIDTH
    return pl.pallas_call(
        functools.partial(_outproj_kernel, final_norm=final_norm),
        grid=(bsz, s // ts),
        in_specs=[
            pl.BlockSpec((1, w, ts), lambda b, i: (b, 0, i)),
            pl.BlockSpec((1, w, ts), lambda b, i: (b, 0, i)),
            pl.BlockSpec((1, ts, w), lambda b, i: (b, i, 0)),
            pl.BlockSpec((1, ts, d), lambda b, i: (b, i, 0)),
            pl.BlockSpec((1, 1, 3 * d), lambda b, i: (b, 0, 0)),
            pl.BlockSpec((w, 1), lambda b, i: (0, 0)),
            pl.BlockSpec((2 * w, d), lambda b, i: (0, 0)),
            pl.BlockSpec((1, d), lambda b, i: (0, 0)),
        ],
        out_specs=pl.BlockSpec((1, ts, d), lambda b, i: (b, i, 0)),
        out_shape=jax.ShapeDtypeStruct((bsz, s, d), F32),
        compiler_params=pltpu.CompilerParams(
            dimension_semantics=("arbitrary", "arbitrary"), vmem_limit_bytes=VMEM_LIMIT),
        name="outproj",
    )(at, zat, yl, x, mod3, gain_a, w_out, final_gain)


def _rope_tables(s):
    pos = np.arange(s, dtype=np.float64)
    inv_freq = ROPE_THETA ** (-np.arange(0, HEAD_DIM, 2, dtype=np.float64) / HEAD_DIM)
    ang = pos[:, None] * inv_freq[None, :]
    cos, sin = np.cos(ang), np.sin(ang)
    cosn = np.tile(cos, (1, LANES // HALF_DIM))
    sinn = np.tile(np.concatenate([-sin, sin], axis=1), (1, LANES // HEAD_DIM))
    cost = np.concatenate([cos.T, cos.T], axis=0)
    sint = np.concatenate([-sin.T, sin.T], axis=0)
    return tuple(jnp.asarray(t, dtype=F32) for t in (cosn, sinn, cost, sint))


def _block_diag(wg):
    g, n, _ = wg.shape
    eye = jnp.eye(g, dtype=wg.dtype)
    return jnp.einsum("gij,gh->gihj", wg, eye).reshape(g * n, g * n)


def kernel(x, c, w_mod, b_mod, norm_gain, w_in, conv_w, conv_b, w_rgate, b_rgate,
           w_igate, b_igate, lru_lambda, attn_out_gain, lru_out_gain, w_out, final_gain):
    bsz, s, d = x.shape
    depth = w_in.shape[0]
    w = ATTN_WIDTH
    assert s % OUT_TILE == 0 and d % LANES == 0 and w_in.shape[2] == 6 * w
    cosn, sinn, cost, sint = _rope_tables(s)
    for l in range(depth):
        mod3 = _modulation(c, w_mod[l], b_mod[l]).reshape(bsz, 1, 3 * d)
        wb = w_in[l].astype(BF16)
        w_t = jnp.concatenate([wb[:, 0:w], wb[:, 2 * w:4 * w]], axis=1).T
        qt, vt, zat, k, yl = _inproj_lru(
            x, mod3, norm_gain[l].reshape(1, d), wb, w_t, cosn, sinn, cost, sint,
            conv_w[l], conv_b[l],
            _block_diag(w_rgate[l].astype(BF16)), _block_diag(w_igate[l].astype(BF16)),
            b_rgate[l].reshape(-1), b_igate[l].reshape(-1), lru_lambda[l], lru_out_gain[l],
            IN_TILE)
        at = _attention(qt, k, vt)
        x = _outproj(at, zat, yl, x, mod3, attn_out_gain[l].reshape(w, 1),
                     w_out[l].astype(BF16), final_gain.reshape(1, d), OUT_TILE,
                     final_norm=(l == depth - 1))
    return x
```

```python
import functools

import jax
import jax.numpy as jnp
import numpy as np
from jax import lax
from jax.experimental import pallas as pl
from jax.experimental.pallas import tpu as pltpu

ATTN_HEADS = 8
HEAD_DIM = 64
HALF_DIM = HEAD_DIM // 2
ATTN_WIDTH = ATTN_HEADS * HEAD_DIM
LRU_BLOCKS = 8
MOBA_BLOCK = 256
MOBA_TOPK = 3
CONV_WIDTH = 4
LRU_C = 8.0
ROPE_THETA = 10000.0
EPS = 1e-6

LANES = 128
SUBLANES = 8
BF16_ROWS = 16
MASK_VALUE = -1e30
LOG2_E = 1.4426950408889634
VMEM_LIMIT = 56 * 1024 * 1024

F32 = jnp.float32
BF16 = jnp.bfloat16


def _sigmoid(z):
    return 0.5 * jnp.tanh(0.5 * z) + 0.5


def _split3(a):
    hi = a.astype(BF16)
    r1 = a - hi.astype(F32)
    mid = r1.astype(BF16)
    lo = (r1 - mid.astype(F32)).astype(BF16)
    return hi, mid, lo


def _mod_kernel(c_ref, w_ref, b_ref, o_ref):
    c = c_ref[...]
    act = c * _sigmoid(c)
    a_hi, a_mid, _ = _split3(act)
    w_hi, w_mid, _ = _split3(w_ref[...])
    dot = functools.partial(jnp.dot, preferred_element_type=F32)
    acc = dot(a_hi, w_hi) + dot(a_hi, w_mid) + dot(a_mid, w_hi)
    o_ref[...] = acc + b_ref[...]


def _modulation(c, w_mod, b_mod):
    bsz, d = c.shape
    n = w_mod.shape[1]
    tn = 512
    return pl.pallas_call(
        _mod_kernel,
        grid=(n // tn,),
        in_specs=[
            pl.BlockSpec((bsz, d), lambda j: (0, 0)),
            pl.BlockSpec((d, tn), lambda j: (0, j)),
            pl.BlockSpec((1, tn), lambda j: (0, j)),
        ],
        out_specs=pl.BlockSpec((bsz, tn), lambda j: (0, j)),
        out_shape=jax.ShapeDtypeStruct((bsz, n), F32),
        compiler_params=pltpu.CompilerParams(
            dimension_semantics=("arbitrary",), vmem_limit_bytes=VMEM_LIMIT),
        name="modulation",
    )(c, w_mod, b_mod.reshape(1, n))


IN_TILE = 1024
OUT_TILE = 2048
OUT_SUB = 256
K_COL = 1 * ATTN_WIDTH
XL_COL = 4 * ATTN_WIDTH
ZL_COL = 5 * ATTN_WIDTH
LRU_CHUNK = 128
PROJ_PIECE = 256
LRU_QUAD = 256


def _inproj_lru_kernel(x_ref, mod_ref, gain_ref, wn_ref, wt_ref, cosn_ref, sinn_ref,
                       cost_ref, sint_ref, cw_ref, cb_ref, wr_ref, wi_ref, br_ref, bi_ref,
                       lam_ref, lg_ref,
                       qt_ref, vt_ref, zat_ref, k_ref, yl_ref,
                       ext_buf, zl_buf, a_buf, u_buf, hc_ref):
    d = x_ref.shape[2]
    ts = x_ref.shape[1]
    w = ATTN_WIDTH
    pad = SUBLANES
    z0 = jnp.minimum(pl.program_id(1), 0)
    ext_ref, zl_ref, a_ref, u_ref = ext_buf.at[z0], zl_buf.at[z0], a_buf.at[z0], u_buf.at[z0]

    @pl.when(pl.program_id(1) == 0)
    def _():
        ext_ref[0:pad, :] = jnp.zeros((pad, w), F32)
        hc_ref[...] = jnp.zeros(hc_ref.shape, F32)

    x = x_ref[0]
    shift = mod_ref[0, :, 0:d]
    scale = mod_ref[0, :, d:2 * d]
    rs = lax.rsqrt(jnp.mean(x * x, axis=-1, keepdims=True) + EPS)
    h = (x * rs) * (gain_ref[...] * (1.0 + scale)) + shift
    hb = h.astype(BF16)

    ext_ref[pad:pad + ts, :] = jnp.dot(hb, wn_ref[:, XL_COL:XL_COL + w],
                                       preferred_element_type=F32)
    zl_ref[...] = jnp.dot(hb, wn_ref[:, ZL_COL:ZL_COL + w], preferred_element_type=F32)

    nt = (((1,), (1,)), ((), ()))
    half = PROJ_PIECE
    n_piece = w // PROJ_PIECE

    def k_piece(p):
        kk = jnp.dot(hb, wn_ref[:, K_COL + p * half:K_COL + (p + 1) * half],
                     preferred_element_type=F32)
        cosn = cosn_ref[...]
        sinn = sinn_ref[...]
        lane = lax.broadcasted_iota(jnp.int32, cosn.shape, 1)
        first_half = (lane % HEAD_DIM) < HALF_DIM
        for g in range(half // LANES):
            kg = kk[:, g * LANES:(g + 1) * LANES]
            swapped = jnp.where(first_half,
                                pltpu.roll(kg, LANES - HALF_DIM, axis=1),
                                pltpu.roll(kg, HALF_DIM, axis=1))
            lo = p * half + g * LANES
            k_ref[0, :, lo:lo + LANES] = (kg * cosn + swapped * sinn).astype(BF16)

    def q_piece(p):
        qt = lax.dot_general(wt_ref[p * half:(p + 1) * half, :], hb, nt,
                             preferred_element_type=F32)
        cost = cost_ref[...]
        sint = sint_ref[...]
        qscale = HEAD_DIM ** -0.5 * LOG2_E
        for hd in range(half // HEAD_DIM):
            t = qt[hd * HEAD_DIM:(hd + 1) * HEAD_DIM]
            swapped = jnp.concatenate([t[HALF_DIM:], t[:HALF_DIM]], axis=0)
            lo = p * half + hd * HEAD_DIM
            qt_ref[0, lo:lo + HEAD_DIM, :] = ((t * cost + swapped * sint) * qscale).astype(BF16)

    def fm_piece(out_ref, base, p):
        lo = p * half
        out_ref[0, lo:lo + half, :] = lax.dot_general(
            wt_ref[base + lo:base + lo + half, :], hb, nt,
            preferred_element_type=F32).astype(BF16)

    lam = lam_ref[...]
    neg_lam = -lam
    softplus = jnp.maximum(neg_lam, 0.0) + jnp.log1p(jnp.exp(-jnp.abs(neg_lam)))
    decay = (-LRU_C * LOG2_E) * softplus
    state = {"h": hc_ref[...]}

    def quad_dot(xb, w_ref):
        return jnp.concatenate(
            [jnp.dot(xb[:, q:q + LRU_QUAD], w_ref[q:q + LRU_QUAD, q:q + LRU_QUAD],
                     preferred_element_type=F32) for q in range(0, w, LRU_QUAD)], axis=1)

    def lru_gates(c):
        base = pad + c * LRU_CHUNK
        xc = cb_ref[...] + cw_ref[CONV_WIDTH - 1:CONV_WIDTH, :] * ext_ref[base:base + LRU_CHUNK, :]
        for tap in range(1, CONV_WIDTH):
            xc = xc + (cw_ref[CONV_WIDTH - 1 - tap:CONV_WIDTH - tap, :] *
                       ext_ref[base - tap:base - tap + LRU_CHUNK, :])
        xcb = xc.astype(BF16)
        r = _sigmoid(quad_dot(xcb, wr_ref) + br_ref[...])
        ig = _sigmoid(quad_dot(xcb, wi_ref) + bi_ref[...])
        a = jnp.exp2(decay * r)
        y = 1.0 - a * a
        u = jnp.where(y > 0.0, y * lax.rsqrt(y), 0.0) * ig * xc
        grp = (LRU_CHUNK // SUBLANES, SUBLANES, w)
        a = a.reshape(grp)
        u = u.reshape(grp)
        rowg = lax.broadcasted_iota(jnp.int32, grp, 1)
        for sh in (1, 2, 4):
            ok = rowg >= sh
            a_prev = pltpu.roll(a, sh, axis=1)
            u_prev = pltpu.roll(u, sh, axis=1)
            u = jnp.where(ok, a * u_prev + u, u)
            a = jnp.where(ok, a * a_prev, a)
        a_ref[c * LRU_CHUNK:(c + 1) * LRU_CHUNK, :] = a.reshape(LRU_CHUNK, w)
        u_ref[c * LRU_CHUNK:(c + 1) * LRU_CHUNK, :] = u.reshape(LRU_CHUNK, w)

    def lru_carry(c):
        hprev = state["h"]
        for g in range(LRU_CHUNK // SUBLANES):
            r0 = c * LRU_CHUNK + g * SUBLANES
            hcur = u_ref[r0:r0 + SUBLANES, :] + a_ref[r0:r0 + SUBLANES, :] * hprev
            u_ref[r0:r0 + SUBLANES, :] = hcur
            hprev = hcur[SUBLANES - 1:SUBLANES, :]
        state["h"] = hprev

    def lru_finish(c):
        rows = slice(c * LRU_CHUNK, (c + 1) * LRU_CHUNK)
        rec = u_ref[rows, :]
        rsq = lax.rsqrt(jnp.mean(rec * rec, axis=-1, keepdims=True) + EPS)
        z = zl_ref[rows, :]
        yl_ref[0, rows, :] = ((rec * rsq) * lg_ref[...] * (z * _sigmoid(z))).astype(BF16)

    mxu_pieces = ([functools.partial(k_piece, p) for p in range(n_piece)] +
                  [functools.partial(q_piece, p) for p in range(n_piece)] +
                  [functools.partial(fm_piece, vt_ref, w, p) for p in range(n_piece)] +
                  [functools.partial(fm_piece, zat_ref, 2 * w, p) for p in range(n_piece)])
    lru_pieces = []
    for c in range(ts // LRU_CHUNK):
        lru_pieces += [functools.partial(lru_gates, c), functools.partial(lru_carry, c),
                       functools.partial(lru_finish, c)]
    done = 0
    for i, piece in enumerate(mxu_pieces):
        piece()
        upto = (i + 1) * len(lru_pieces) // len(mxu_pieces)
        for lp in lru_pieces[done:upto]:
            lp()
        done = upto

    hc_ref[...] = state["h"]
    ext_ref[0:pad, :] = ext_ref[ts:ts + pad, :]


def _inproj_lru(x, mod3, gain, w_nat, w_t, cosn, sinn, cost, sint,
                conv_w, conv_b, wr_bd, wi_bd, b_r, b_i, lam, lru_gain, ts):
    bsz, s, d = x.shape
    w = ATTN_WIDTH
    fm = jax.ShapeDtypeStruct((bsz, w, s), BF16)
    sm = jax.ShapeDtypeStruct((bsz, s, w), BF16)
    fm_spec = pl.BlockSpec((1, w, ts), lambda b, i: (b, 0, i))
    sm_spec = pl.BlockSpec((1, ts, w), lambda b, i: (b, i, 0))
    vec_spec = pl.BlockSpec((1, w), lambda b, i: (0, 0))
    mat_spec = pl.BlockSpec((w, w), lambda b, i: (0, 0))
    row = lambda a: a.reshape(1, w)
    return pl.pallas_call(
        _inproj_lru_kernel,
        grid=(bsz, s // ts),
        in_specs=[
            pl.BlockSpec((1, ts, d), lambda b, i: (b, i, 0)),
            pl.BlockSpec((1, 1, 3 * d), lambda b, i: (b, 0, 0)),
            pl.BlockSpec((1, d), lambda b, i: (0, 0)),
            pl.BlockSpec((d, 6 * w), lambda b, i: (0, 0)),
            pl.BlockSpec((3 * w, d), lambda b, i: (0, 0)),
            pl.BlockSpec((ts, LANES), lambda b, i: (i, 0)),
            pl.BlockSpec((ts, LANES), lambda b, i: (i, 0)),
            pl.BlockSpec((HEAD_DIM, ts), lambda b, i: (0, i)),
            pl.BlockSpec((HEAD_DIM, ts), lambda b, i: (0, i)),
            pl.BlockSpec((CONV_WIDTH, w), lambda b, i: (0, 0)), vec_spec,
            mat_spec, mat_spec, vec_spec, vec_spec, vec_spec, vec_spec,
        ],
        out_specs=[fm_spec, fm_spec, fm_spec, sm_spec, sm_spec],
        out_shape=[fm, fm, fm, sm, sm],
        scratch_shapes=[pltpu.VMEM((1, ts + SUBLANES, w), F32),
                        pltpu.VMEM((1, ts, w), F32),
                        pltpu.VMEM((1, ts, w), F32),
                        pltpu.VMEM((1, ts, w), F32),
                        pltpu.VMEM((1, w), F32)],
        compiler_params=pltpu.CompilerParams(
            dimension_semantics=("arbitrary", "arbitrary"), vmem_limit_bytes=VMEM_LIMIT),
        name="inproj_rglru",
    )(x, mod3, gain, w_nat, w_t, cosn, sinn, cost, sint,
      conv_w, row(conv_b), wr_bd, wi_bd, row(b_r), row(b_i), row(lam), row(lru_gain))


HEADS_PER_GROUP = LANES // HEAD_DIM
HEADS_PER_STEP = 4
BLOCK_SHIFT = MOBA_BLOCK.bit_length() - 1
assert 1 << BLOCK_SHIFT == MOBA_BLOCK
ATTN_LEAD = 5
ATTN_RING = 16


def _attn_kernel(qt_ref, k_ref, vt_ref, zat_ref, ga_ref, oh_ref, avg_ref, o_ref, ssq_ref,
                 ka_ref, qa_ref, va_ref, s_ref):
    s = qt_ref.shape[2]
    nblk = s // MOBA_BLOCK

    jrow = lax.broadcasted_iota(jnp.int32, (nblk, s), 0)
    qblk = lax.shift_right_logical(lax.broadcasted_iota(jnp.int32, (nblk, s), 1), BLOCK_SHIFT)
    cand = jrow < qblk
    qb = qblk[0:1, :]
    kr = lax.broadcasted_iota(jnp.int32, (MOBA_BLOCK, MOBA_BLOCK), 0)
    qc = lax.broadcasted_iota(jnp.int32, (MOBA_BLOCK, MOBA_BLOCK), 1)
    causal = kr <= qc

    for hh in range(HEADS_PER_STEP):
        grp, hp = divmod(hh, HEADS_PER_GROUP)
        grp_lanes = slice(grp * LANES, (grp + 1) * LANES)
        head_lo = hp * HEAD_DIM
        aux_lo = HEAD_DIM - head_lo
        rows_l = slice(head_lo, head_lo + HEAD_DIM)
        rows_h = slice(hh * HEAD_DIM, (hh + 1) * HEAD_DIM)

        if hp == 0:
            kmean = jnp.dot(avg_ref[...], k_ref[0, :, grp_lanes], preferred_element_type=F32)
            km3 = jnp.concatenate(_split3(kmean), axis=0)

        ka_ref[hh] = oh_ref[hp]
        ka_ref[hh, :, rows_l] = k_ref[0, :, grp * LANES + head_lo:grp * LANES + head_lo + HEAD_DIM]

        va_ref[hh, 0:HEAD_DIM, :] = vt_ref[0, rows_h, :]
        va_ref[hh, HEAD_DIM:, :] = jnp.ones((BF16_ROWS, s), BF16)

        qa_ref[hh] = jnp.zeros(qa_ref.shape[1:], BF16)
        qa_ref[hh, rows_l, :] = qt_ref[0, rows_h, :]
        g3 = jnp.dot(km3, qa_ref[hh], preferred_element_type=F32)
        gate = (g3[0:BF16_ROWS] + g3[BF16_ROWS:2 * BF16_ROWS]) + g3[2 * BF16_ROWS:]
        gate = gate[0:nblk]

        bias_rows = []
        for j in range(nblk):
            gj = gate[j:j + 1, :]
            beats = cand & ((gate > gj) | ((gate == gj) & (jrow < j)))
            rank = jnp.sum(beats.astype(F32), axis=0, keepdims=True)
            keep = ((qb > j) & (rank < MOBA_TOPK)) | (qb == j)
            bias_rows.append(jnp.where(keep, 0.0, MASK_VALUE))
        bias = jnp.concatenate(bias_rows + [jnp.zeros((BF16_ROWS - nblk, s), F32)], axis=0)
        qa_ref[hh, aux_lo:aux_lo + BF16_ROWS, :] = bias.astype(BF16)

    items = [(i, hh) for hh in range(HEADS_PER_STEP)
             for i in (range(nblk) if hh % 2 == 0 else reversed(range(nblk)))]
    piece_list = [(n, j) for n, (i, _) in enumerate(items) for j in range(i + 1)]
    item_end = {}
    for t, (n, _) in enumerate(piece_list):
        item_end[n] = t + 1
    state = [dict() for _ in items]
    ring = s_ref.shape[0]
    slot0 = jnp.minimum(pl.program_id(1), 0)

    def blk_rows(j):
        return slice(j * MOBA_BLOCK, (j + 1) * MOBA_BLOCK)

    def score_piece(t):
        n, j = piece_list[t]
        i, hh = items[n]
        sc = jnp.dot(ka_ref[hh, blk_rows(j), :], qa_ref[hh, :, blk_rows(i)],
                     preferred_element_type=F32)
        if j == i:
            sc = jnp.where(causal, sc, MASK_VALUE)
        s_ref[slot0 + t % ring] = sc
        cm = jnp.max(sc, axis=0, keepdims=True)
        st = state[n]
        st["m"] = cm if j == 0 else jnp.maximum(st["m"], cm)

    def softmax_piece(t):
        n, j = piece_list[t]
        i, hh = items[n]
        rows_h = slice(hh * HEAD_DIM, (hh + 1) * HEAD_DIM)
        st = state[n]
        p = jnp.exp2(s_ref[slot0 + t % ring] - st["m"])
        pv = jnp.dot(va_ref[hh, :, blk_rows(j)], p.astype(BF16),
                     preferred_element_type=F32)
        st["acc"] = pv if j == 0 else st["acc"] + pv
        if j == i:
            acc = st["acc"]
            o = acc[0:HEAD_DIM] / acc[HEAD_DIM:HEAD_DIM + 1]
            ssq = jnp.sum(o * o, axis=0, keepdims=True)
            sumsq[i] = ssq if i not in sumsq else sumsq[i] + ssq
            z = zat_ref[0, rows_h, blk_rows(i)].astype(F32)
            o_ref[0, rows_h, blk_rows(i)] = (
                o * (ga_ref[rows_h, :] * (z * _sigmoid(z)))).astype(BF16)

    total = len(piece_list)
    scored = softmaxed = 0
    sumsq = {}
    while softmaxed < total:
        if scored < total:
            assert scored - softmaxed < ring, "score ring too small for the softmax lag"
            score_piece(scored)
            scored += 1
        ready = min(total, item_end[piece_list[softmaxed][0]] + ATTN_LEAD)
        if scored >= ready:
            softmax_piece(softmaxed)
            softmaxed += 1

    ssq_ref[0, 1:SUBLANES, :] = jnp.zeros((SUBLANES - 1, s), F32)
    for i in range(nblk):
        ssq_ref[0, 0:1, blk_rows(i)] = sumsq[i]


def _attention(qt, k, vt, zat, gain_a):
    bsz, w, s = qt.shape
    blk = np.arange(s) // MOBA_BLOCK
    lane = np.arange(LANES)
    oh = jnp.asarray(np.stack([(lane[None, :] == (HEAD_DIM - hp * HEAD_DIM) + blk[:, None])
                               for hp in range(HEADS_PER_GROUP)]), dtype=BF16)
    avg = jnp.asarray(np.where(np.arange(BF16_ROWS)[:, None] == blk[None, :],
                               1.0 / MOBA_BLOCK, 0.0), dtype=BF16)
    step_w = HEADS_PER_STEP * HEAD_DIM
    n_steps = ATTN_HEADS // HEADS_PER_STEP
    pair_fm = pl.BlockSpec((1, step_w, s), lambda b, p: (b, p, 0))
    return pl.pallas_call(
        _attn_kernel,
        grid=(bsz, n_steps),
        in_specs=[
            pair_fm,
            pl.BlockSpec((1, s, step_w), lambda b, p: (b, 0, p)),
            pair_fm,
            pair_fm,
            pl.BlockSpec((step_w, 1), lambda b, p: (p, 0)),
            pl.BlockSpec((HEADS_PER_GROUP, s, LANES), lambda b, p: (0, 0, 0)),
            pl.BlockSpec((BF16_ROWS, s), lambda b, p: (0, 0)),
        ],
        out_specs=[pair_fm, pl.BlockSpec((1, SUBLANES, s), lambda b, p: (b, p, 0))],
        out_shape=[jax.ShapeDtypeStruct((bsz, w, s), BF16),
                   jax.ShapeDtypeStruct((bsz, SUBLANES * n_steps, s), F32)],
        scratch_shapes=[pltpu.VMEM((HEADS_PER_STEP, s, LANES), BF16),
                        pltpu.VMEM((HEADS_PER_STEP, LANES, s), BF16),
                        pltpu.VMEM((HEADS_PER_STEP, HEAD_DIM + BF16_ROWS, s), BF16),
                        pltpu.VMEM((ATTN_RING, MOBA_BLOCK, MOBA_BLOCK), F32)],
        compiler_params=pltpu.CompilerParams(
            dimension_semantics=("arbitrary", "arbitrary"), vmem_limit_bytes=VMEM_LIMIT),
        name="moba_attention",
    )(qt, k, vt, zat, gain_a, oh, avg)


def _outproj_kernel(ag_ref, ssq_ref, yl_ref, x_ref, mod_ref, wo_ref, fg_ref, o_ref,
                    *, final_norm):
    d = x_ref.shape[2]
    ts = x_ref.shape[1]
    w = ATTN_WIDTH
    gate = mod_ref[0, :, 2 * d:3 * d]
    tn = (((0,), (0,)), ((), ()))
    subs = [slice(r, r + OUT_SUB) for r in range(0, ts, OUT_SUB)]
    ya, y = {}, {}

    def prep(i):
        ag = ag_ref[0, :, subs[i]].astype(F32)
        ms = jnp.sum(ssq_ref[0, :, subs[i]], axis=0, keepdims=True) * (1.0 / w)
        ya[i] = (ag * lax.rsqrt(ms + EPS)).astype(BF16)

    def project(i):
        yy = lax.dot_general(ya[i], wo_ref[0:w, :], tn, preferred_element_type=F32)
        y[i] = yy + jnp.dot(yl_ref[0, subs[i], :], wo_ref[w:2 * w, :],
                            preferred_element_type=F32)

    def finish(i):
        xo = x_ref[0, subs[i], :] + gate * y[i]
        if final_norm:
            xo = (xo * lax.rsqrt(jnp.mean(xo * xo, axis=-1, keepdims=True) + EPS)) * fg_ref[...]
        o_ref[0, subs[i], :] = xo

    n = len(subs)
    prep(0)
    for i in range(n):
        if i + 1 < n:
            prep(i + 1)
        project(i)
        if i > 0:
            finish(i - 1)
    finish(n - 1)


def _outproj(ag, ssq, yl, x, mod3, w_out, final_gain, ts, final_norm):
    bsz, s, d = x.shape
    w = ATTN_WIDTH
    return pl.pallas_call(
        functools.partial(_outproj_kernel, final_norm=final_norm),
        grid=(bsz, s // ts),
        in_specs=[
            pl.BlockSpec((1, w, ts), lambda b, i: (b, 0, i)),
            pl.BlockSpec((1, ssq.shape[1], ts), lambda b, i: (b, 0, i)),
            pl.BlockSpec((1, ts, w), lambda b, i: (b, i, 0)),
            pl.BlockSpec((1, ts, d), lambda b, i: (b, i, 0)),
            pl.BlockSpec((1, 1, 3 * d), lambda b, i: (b, 0, 0)),
            pl.BlockSpec((2 * w, d), lambda b, i: (0, 0)),
            pl.BlockSpec((1, d), lambda b, i: (0, 0)),
        ],
        out_specs=pl.BlockSpec((1, ts, d), lambda b, i: (b, i, 0)),
        out_shape=jax.ShapeDtypeStruct((bsz, s, d), F32),
        compiler_params=pltpu.CompilerParams(
            dimension_semantics=("arbitrary", "arbitrary"), vmem_limit_bytes=VMEM_LIMIT),
        name="outproj",
    )(ag, ssq, yl, x, mod3, w_out, final_gain)


def _rope_tables(s):
    pos = np.arange(s, dtype=np.float64)
    inv_freq = ROPE_THETA ** (-np.arange(0, HEAD_DIM, 2, dtype=np.float64) / HEAD_DIM)
    ang = pos[:, None] * inv_freq[None, :]
    cos, sin = np.cos(ang), np.sin(ang)
    cosn = np.tile(cos, (1, LANES // HALF_DIM))
    sinn = np.tile(np.concatenate([-sin, sin], axis=1), (1, LANES // HEAD_DIM))
    cost = np.concatenate([cos.T, cos.T], axis=0)
    sint = np.concatenate([-sin.T, sin.T], axis=0)
    return tuple(jnp.asarray(t, dtype=F32) for t in (cosn, sinn, cost, sint))


def _block_diag(wg):
    g, n, _ = wg.shape
    eye = jnp.eye(g, dtype=wg.dtype)
    return jnp.einsum("gij,gh->gihj", wg, eye).reshape(g * n, g * n)


def kernel(x, c, w_mod, b_mod, norm_gain, w_in, conv_w, conv_b, w_rgate, b_rgate,
           w_igate, b_igate, lru_lambda, attn_out_gain, lru_out_gain, w_out, final_gain):
    bsz, s, d = x.shape
    depth = w_in.shape[0]
    w = ATTN_WIDTH
    assert s % OUT_TILE == 0 and d % LANES == 0 and w_in.shape[2] == 6 * w
    cosn, sinn, cost, sint = _rope_tables(s)
    for l in range(depth):
        mod3 = _modulation(c, w_mod[l], b_mod[l]).reshape(bsz, 1, 3 * d)
        wb = w_in[l].astype(BF16)
        w_t = jnp.concatenate([wb[:, 0:w], wb[:, 2 * w:4 * w]], axis=1).T
        qt, vt, zat, k, yl = _inproj_lru(
            x, mod3, norm_gain[l].reshape(1, d), wb, w_t, cosn, sinn, cost, sint,
            conv_w[l], conv_b[l],
            _block_diag(w_rgate[l].astype(BF16)), _block_diag(w_igate[l].astype(BF16)),
            b_rgate[l].reshape(-1), b_igate[l].reshape(-1), lru_lambda[l], lru_out_gain[l],
            IN_TILE)
        ag, ssq = _attention(qt, k, vt, zat, attn_out_gain[l].reshape(w, 1))
        x = _outproj(ag, ssq, yl, x, mod3, w_out[l].astype(BF16), final_gain.reshape(1, d),
                     OUT_TILE, final_norm=(l == depth - 1))
    return x
```

```python
import functools

import jax
import jax.numpy as jnp
import numpy as np
from jax import lax
from jax.experimental import pallas as pl
from jax.experimental.pallas import tpu as pltpu

ATTN_HEADS = 8
HEAD_DIM = 64
HALF_DIM = HEAD_DIM // 2
ATTN_WIDTH = ATTN_HEADS * HEAD_DIM
LRU_BLOCKS = 8
MOBA_BLOCK = 256
MOBA_TOPK = 3
CONV_WIDTH = 4
LRU_C = 8.0
ROPE_THETA = 10000.0
EPS = 1e-6

LANES = 128
SUBLANES = 8
BF16_ROWS = 16
MASK_VALUE = -1e30
LOG2_E = 1.4426950408889634
VMEM_LIMIT = 56 * 1024 * 1024

F32 = jnp.float32
BF16 = jnp.bfloat16


def _sigmoid(z):
    return 0.5 * jnp.tanh(0.5 * z) + 0.5


def _split3(a):
    hi = a.astype(BF16)
    r1 = a - hi.astype(F32)
    mid = r1.astype(BF16)
    lo = (r1 - mid.astype(F32)).astype(BF16)
    return hi, mid, lo


def _mod_kernel(c_ref, w_ref, b_ref, o_ref):
    c = c_ref[...]
    act = c * _sigmoid(c)
    a_hi, a_mid, _ = _split3(act)
    w_hi, w_mid, _ = _split3(w_ref[...])
    dot = functools.partial(jnp.dot, preferred_element_type=F32)
    acc = dot(a_hi, w_hi) + dot(a_hi, w_mid) + dot(a_mid, w_hi)
    o_ref[...] = acc + b_ref[...]


def _modulation(c, w_mod, b_mod):
    bsz, d = c.shape
    n = w_mod.shape[1]
    tn = 1024
    return pl.pallas_call(
        _mod_kernel,
        grid=(n // tn,),
        in_specs=[
            pl.BlockSpec((bsz, d), lambda j: (0, 0)),
            pl.BlockSpec((d, tn), lambda j: (0, j)),
            pl.BlockSpec((1, tn), lambda j: (0, j)),
        ],
        out_specs=pl.BlockSpec((bsz, tn), lambda j: (0, j)),
        out_shape=jax.ShapeDtypeStruct((bsz, n), F32),
        compiler_params=pltpu.CompilerParams(
            dimension_semantics=("arbitrary",), vmem_limit_bytes=VMEM_LIMIT),
        name="modulation",
    )(c, w_mod, b_mod.reshape(1, n))


def _wprep_kernel(w_ref, wb_ref, wt_ref):
    j = pl.program_id(0)
    wblk = w_ref[...]
    wb_ref[...] = wblk.astype(BF16)

    @pl.when((j == Q_GROUP) | (j == V_GROUP) | (j == ZA_GROUP))
    def _():
        wt_ref[...] = wblk.T.astype(BF16)


def _weight_prep(w_in):
    d, n = w_in.shape
    w = ATTN_WIDTH
    t_block = lambda j: jnp.clip(j - 1, 0, 2)
    return pl.pallas_call(
        _wprep_kernel,
        grid=(n // w,),
        in_specs=[pl.BlockSpec((d, w), lambda j: (0, j))],
        out_specs=[pl.BlockSpec((d, w), lambda j: (0, j)),
                   pl.BlockSpec((w, d), lambda j: (t_block(j), 0))],
        out_shape=[jax.ShapeDtypeStruct((d, n), BF16), jax.ShapeDtypeStruct((3 * w, d), BF16)],
        compiler_params=pltpu.CompilerParams(
            dimension_semantics=("arbitrary",), vmem_limit_bytes=VMEM_LIMIT),
        name="weight_prep",
    )(w_in)


Q_GROUP, K_GROUP, V_GROUP, ZA_GROUP, XL_GROUP, ZL_GROUP = range(6)
IN_TILE = 1024
OUT_TILE = 2048
OUT_SUB = 256
K_COL = 1 * ATTN_WIDTH
XL_COL = 4 * ATTN_WIDTH
ZL_COL = 5 * ATTN_WIDTH
LRU_CHUNK = 128
PROJ_PIECE = 256
LRU_QUAD = 256


def _inproj_lru_kernel(x_ref, mod_ref, gain_ref, wn_ref, wt_ref, cosn_ref, sinn_ref,
                       cost_ref, sint_ref, cw_ref, cb_ref, wr_ref, wi_ref, br_ref, bi_ref,
                       lam_ref, lg_ref,
                       qt_ref, vt_ref, zat_ref, k_ref, yl_ref,
                       ext_buf, zl_buf, a_buf, u_buf, hc_ref):
    d = x_ref.shape[2]
    ts = x_ref.shape[1]
    w = ATTN_WIDTH
    pad = SUBLANES
    z0 = jnp.minimum(pl.program_id(1), 0)
    ext_ref, zl_ref, a_ref, u_ref = ext_buf.at[z0], zl_buf.at[z0], a_buf.at[z0], u_buf.at[z0]

    @pl.when(pl.program_id(1) == 0)
    def _():
        ext_ref[0:pad, :] = jnp.zeros((pad, w), F32)
        hc_ref[...] = jnp.zeros(hc_ref.shape, F32)

    x = x_ref[0]
    shift = mod_ref[0, :, 0:d]
    scale = mod_ref[0, :, d:2 * d]
    rs = lax.rsqrt(jnp.mean(x * x, axis=-1, keepdims=True) + EPS)
    h = (x * rs) * (gain_ref[...] * (1.0 + scale)) + shift
    hb = h.astype(BF16)

    ext_ref[pad:pad + ts, :] = jnp.dot(hb, wn_ref[:, XL_COL:XL_COL + w],
                                       preferred_element_type=F32)
    zl_ref[...] = jnp.dot(hb, wn_ref[:, ZL_COL:ZL_COL + w], preferred_element_type=F32)

    nt = (((1,), (1,)), ((), ()))
    half = PROJ_PIECE
    n_piece = w // PROJ_PIECE

    def k_piece(p):
        kk = jnp.dot(hb, wn_ref[:, K_COL + p * half:K_COL + (p + 1) * half],
                     preferred_element_type=F32)
        cosn = cosn_ref[...]
        sinn = sinn_ref[...]
        lane = lax.broadcasted_iota(jnp.int32, cosn.shape, 1)
        first_half = (lane % HEAD_DIM) < HALF_DIM
        for g in range(half // LANES):
            kg = kk[:, g * LANES:(g + 1) * LANES]
            swapped = jnp.where(first_half,
                                pltpu.roll(kg, LANES - HALF_DIM, axis=1),
                                pltpu.roll(kg, HALF_DIM, axis=1))
            lo = p * half + g * LANES
            k_ref[0, :, lo:lo + LANES] = (kg * cosn + swapped * sinn).astype(BF16)

    def q_piece(p):
        qt = lax.dot_general(wt_ref[p * half:(p + 1) * half, :], hb, nt,
                             preferred_element_type=F32)
        cost = cost_ref[...]
        sint = sint_ref[...]
        qscale = HEAD_DIM ** -0.5 * LOG2_E
        for hd in range(half // HEAD_DIM):
            t = qt[hd * HEAD_DIM:(hd + 1) * HEAD_DIM]
            swapped = jnp.concatenate([t[HALF_DIM:], t[:HALF_DIM]], axis=0)
            lo = p * half + hd * HEAD_DIM
            qt_ref[0, lo:lo + HEAD_DIM, :] = ((t * cost + swapped * sint) * qscale).astype(BF16)

    def fm_piece(out_ref, base, p):
        lo = p * half
        out_ref[0, lo:lo + half, :] = lax.dot_general(
            wt_ref[base + lo:base + lo + half, :], hb, nt,
            preferred_element_type=F32).astype(BF16)

    lam = lam_ref[...]
    neg_lam = -lam
    softplus = jnp.maximum(neg_lam, 0.0) + jnp.log1p(jnp.exp(-jnp.abs(neg_lam)))
    decay = (-LRU_C * LOG2_E) * softplus
    state = {"h": hc_ref[...]}

    def quad_dot(xb, w_ref):
        return jnp.concatenate(
            [jnp.dot(xb[:, q:q + LRU_QUAD], w_ref[q:q + LRU_QUAD, q:q + LRU_QUAD],
                     preferred_element_type=F32) for q in range(0, w, LRU_QUAD)], axis=1)

    def lru_gates(c):
        base = pad + c * LRU_CHUNK
        xc = cb_ref[...] + cw_ref[CONV_WIDTH - 1:CONV_WIDTH, :] * ext_ref[base:base + LRU_CHUNK, :]
        for tap in range(1, CONV_WIDTH):
            xc = xc + (cw_ref[CONV_WIDTH - 1 - tap:CONV_WIDTH - tap, :] *
                       ext_ref[base - tap:base - tap + LRU_CHUNK, :])
        xcb = xc.astype(BF16)
        r = _sigmoid(quad_dot(xcb, wr_ref) + br_ref[...])
        ig = _sigmoid(quad_dot(xcb, wi_ref) + bi_ref[...])
        a = jnp.exp2(decay * r)
        y = 1.0 - a * a
        u = jnp.where(y > 0.0, y * lax.rsqrt(y), 0.0) * ig * xc
        grp = (LRU_CHUNK // SUBLANES, SUBLANES, w)
        a = a.reshape(grp)
        u = u.reshape(grp)
        rowg = lax.broadcasted_iota(jnp.int32, grp, 1)
        for sh in (1, 2, 4):
            ok = rowg >= sh
            a_prev = pltpu.roll(a, sh, axis=1)
            u_prev = pltpu.roll(u, sh, axis=1)
            u = jnp.where(ok, a * u_prev + u, u)
            a = jnp.where(ok, a * a_prev, a)
        a_ref[c * LRU_CHUNK:(c + 1) * LRU_CHUNK, :] = a.reshape(LRU_CHUNK, w)
        u_ref[c * LRU_CHUNK:(c + 1) * LRU_CHUNK, :] = u.reshape(LRU_CHUNK, w)

    def lru_carry(c):
        hprev = state["h"]
        for g in range(LRU_CHUNK // SUBLANES):
            r0 = c * LRU_CHUNK + g * SUBLANES
            hcur = u_ref[r0:r0 + SUBLANES, :] + a_ref[r0:r0 + SUBLANES, :] * hprev
            u_ref[r0:r0 + SUBLANES, :] = hcur
            hprev = hcur[SUBLANES - 1:SUBLANES, :]
        state["h"] = hprev

    def lru_finish(c):
        rows = slice(c * LRU_CHUNK, (c + 1) * LRU_CHUNK)
        rec = u_ref[rows, :]
        rsq = lax.rsqrt(jnp.mean(rec * rec, axis=-1, keepdims=True) + EPS)
        z = zl_ref[rows, :]
        yl_ref[0, rows, :] = ((rec * rsq) * lg_ref[...] * (z * _sigmoid(z))).astype(BF16)

    mxu_pieces = ([functools.partial(k_piece, p) for p in range(n_piece)] +
                  [functools.partial(q_piece, p) for p in range(n_piece)] +
                  [functools.partial(fm_piece, vt_ref, w, p) for p in range(n_piece)] +
                  [functools.partial(fm_piece, zat_ref, 2 * w, p) for p in range(n_piece)])
    lru_pieces = []
    for c in range(ts // LRU_CHUNK):
        lru_pieces += [functools.partial(lru_gates, c), functools.partial(lru_carry, c),
                       functools.partial(lru_finish, c)]
    done = 0
    for i, piece in enumerate(mxu_pieces):
        piece()
        upto = (i + 1) * len(lru_pieces) // len(mxu_pieces)
        for lp in lru_pieces[done:upto]:
            lp()
        done = upto

    hc_ref[...] = state["h"]
    ext_ref[0:pad, :] = ext_ref[ts:ts + pad, :]


def _inproj_lru(x, mod3, gain, w_nat, w_t, cosn, sinn, cost, sint,
                conv_w, conv_b, wr_bd, wi_bd, b_r, b_i, lam, lru_gain, ts):
    bsz, s, d = x.shape
    w = ATTN_WIDTH
    fm = jax.ShapeDtypeStruct((bsz, w, s), BF16)
    sm = jax.ShapeDtypeStruct((bsz, s, w), BF16)
    fm_spec = pl.BlockSpec((1, w, ts), lambda b, i: (b, 0, i))
    sm_spec = pl.BlockSpec((1, ts, w), lambda b, i: (b, i, 0))
    vec_spec = pl.BlockSpec((1, w), lambda b, i: (0, 0))
    mat_spec = pl.BlockSpec((w, w), lambda b, i: (0, 0))
    row = lambda a: a.reshape(1, w)
    return pl.pallas_call(
        _inproj_lru_kernel,
        grid=(bsz, s // ts),
        in_specs=[
            pl.BlockSpec((1, ts, d), lambda b, i: (b, i, 0)),
            pl.BlockSpec((1, 1, 3 * d), lambda b, i: (b, 0, 0)),
            pl.BlockSpec((1, d), lambda b, i: (0, 0)),
            pl.BlockSpec((d, 6 * w), lambda b, i: (0, 0)),
            pl.BlockSpec((3 * w, d), lambda b, i: (0, 0)),
            pl.BlockSpec((ts, LANES), lambda b, i: (i, 0)),
            pl.BlockSpec((ts, LANES), lambda b, i: (i, 0)),
            pl.BlockSpec((HEAD_DIM, ts), lambda b, i: (0, i)),
            pl.BlockSpec((HEAD_DIM, ts), lambda b, i: (0, i)),
            pl.BlockSpec((CONV_WIDTH, w), lambda b, i: (0, 0)), vec_spec,
            mat_spec, mat_spec, vec_spec, vec_spec, vec_spec, vec_spec,
        ],
        out_specs=[fm_spec, fm_spec, fm_spec, sm_spec, sm_spec],
        out_shape=[fm, fm, fm, sm, sm],
        scratch_shapes=[pltpu.VMEM((1, ts + SUBLANES, w), F32),
                        pltpu.VMEM((1, ts, w), F32),
                        pltpu.VMEM((1, ts, w), F32),
                        pltpu.VMEM((1, ts, w), F32),
                        pltpu.VMEM((1, w), F32)],
        compiler_params=pltpu.CompilerParams(
            dimension_semantics=("arbitrary", "arbitrary"), vmem_limit_bytes=VMEM_LIMIT),
        name="inproj_rglru",
    )(x, mod3, gain, w_nat, w_t, cosn, sinn, cost, sint,
      conv_w, row(conv_b), wr_bd, wi_bd, row(b_r), row(b_i), row(lam), row(lru_gain))


HEADS_PER_GROUP = LANES // HEAD_DIM
HEADS_PER_STEP = 4
BLOCK_SHIFT = MOBA_BLOCK.bit_length() - 1
assert 1 << BLOCK_SHIFT == MOBA_BLOCK
ATTN_LEAD = 5
ATTN_RING = 16


def _attn_kernel(qt_ref, k_ref, vt_ref, oh_ref, avg_ref, o_ref, ka_ref, qa_ref, va_ref, s_ref):
    s = qt_ref.shape[2]
    nblk = s // MOBA_BLOCK

    jrow = lax.broadcasted_iota(jnp.int32, (nblk, s), 0)
    qblk = lax.shift_right_logical(lax.broadcasted_iota(jnp.int32, (nblk, s), 1), BLOCK_SHIFT)
    cand = jrow < qblk
    qb = qblk[0:1, :]
    kr = lax.broadcasted_iota(jnp.int32, (MOBA_BLOCK, MOBA_BLOCK), 0)
    qc = lax.broadcasted_iota(jnp.int32, (MOBA_BLOCK, MOBA_BLOCK), 1)
    causal = kr <= qc

    for hh in range(HEADS_PER_STEP):
        grp, hp = divmod(hh, HEADS_PER_GROUP)
        grp_lanes = slice(grp * LANES, (grp + 1) * LANES)
        head_lo = hp * HEAD_DIM
        aux_lo = HEAD_DIM - head_lo
        rows_l = slice(head_lo, head_lo + HEAD_DIM)
        rows_h = slice(hh * HEAD_DIM, (hh + 1) * HEAD_DIM)

        if hp == 0:
            kmean = jnp.dot(avg_ref[...], k_ref[0, :, grp_lanes], preferred_element_type=F32)
            km3 = jnp.concatenate(_split3(kmean), axis=0)

        ka_ref[hh] = oh_ref[hp]
        ka_ref[hh, :, rows_l] = k_ref[0, :, grp * LANES + head_lo:grp * LANES + head_lo + HEAD_DIM]

        va_ref[hh, 0:HEAD_DIM, :] = vt_ref[0, rows_h, :]
        va_ref[hh, HEAD_DIM:, :] = jnp.ones((BF16_ROWS, s), BF16)

        qa_ref[hh] = jnp.zeros(qa_ref.shape[1:], BF16)
        qa_ref[hh, rows_l, :] = qt_ref[0, rows_h, :]
        g3 = jnp.dot(km3, qa_ref[hh], preferred_element_type=F32)
        gate = (g3[0:BF16_ROWS] + g3[BF16_ROWS:2 * BF16_ROWS]) + g3[2 * BF16_ROWS:]
        gate = gate[0:nblk]

        bias_rows = []
        for j in range(nblk):
            gj = gate[j:j + 1, :]
            beats = cand & ((gate > gj) | ((gate == gj) & (jrow < j)))
            rank = jnp.sum(beats.astype(F32), axis=0, keepdims=True)
            keep = ((qb > j) & (rank < MOBA_TOPK)) | (qb == j)
            bias_rows.append(jnp.where(keep, 0.0, MASK_VALUE))
        bias = jnp.concatenate(bias_rows + [jnp.zeros((BF16_ROWS - nblk, s), F32)], axis=0)
        qa_ref[hh, aux_lo:aux_lo + BF16_ROWS, :] = bias.astype(BF16)

    items = [(i, hh) for hh in range(HEADS_PER_STEP)
             for i in (range(nblk) if hh % 2 == 0 else reversed(range(nblk)))]
    piece_list = [(n, j) for n, (i, _) in enumerate(items) for j in range(i + 1)]
    item_end = {}
    for t, (n, _) in enumerate(piece_list):
        item_end[n] = t + 1
    state = [dict() for _ in items]
    ring = s_ref.shape[0]
    slot0 = jnp.minimum(pl.program_id(1), 0)

    def blk_rows(j):
        return slice(j * MOBA_BLOCK, (j + 1) * MOBA_BLOCK)

    def score_piece(t):
        n, j = piece_list[t]
        i, hh = items[n]
        sc = jnp.dot(ka_ref[hh, blk_rows(j), :], qa_ref[hh, :, blk_rows(i)],
                     preferred_element_type=F32)
        if j == i:
            sc = jnp.where(causal, sc, MASK_VALUE)
        s_ref[slot0 + t % ring] = sc
        cm = jnp.max(sc, axis=0, keepdims=True)
        st = state[n]
        st["m"] = cm if j == 0 else jnp.maximum(st["m"], cm)

    def softmax_piece(t):
        n, j = piece_list[t]
        i, hh = items[n]
        rows_h = slice(hh * HEAD_DIM, (hh + 1) * HEAD_DIM)
        st = state[n]
        p = jnp.exp2(s_ref[slot0 + t % ring] - st["m"])
        pv = jnp.dot(va_ref[hh, :, blk_rows(j)], p.astype(BF16),
                     preferred_element_type=F32)
        st["acc"] = pv if j == 0 else st["acc"] + pv
        if j == i:
            acc = st["acc"]
            o_ref[0, rows_h, blk_rows(i)] = (
                acc[0:HEAD_DIM] / acc[HEAD_DIM:HEAD_DIM + 1]).astype(BF16)

    total = len(piece_list)
    scored = softmaxed = 0
    while softmaxed < total:
        if scored < total:
            assert scored - softmaxed < ring, "score ring too small for the softmax lag"
            score_piece(scored)
            scored += 1
        ready = min(total, item_end[piece_list[softmaxed][0]] + ATTN_LEAD)
        if scored >= ready:
            softmax_piece(softmaxed)
            softmaxed += 1


def _attention(qt, k, vt):
    bsz, w, s = qt.shape
    blk = np.arange(s) // MOBA_BLOCK
    lane = np.arange(LANES)
    oh = jnp.asarray(np.stack([(lane[None, :] == (HEAD_DIM - hp * HEAD_DIM) + blk[:, None])
                               for hp in range(HEADS_PER_GROUP)]), dtype=BF16)
    avg = jnp.asarray(np.where(np.arange(BF16_ROWS)[:, None] == blk[None, :],
                               1.0 / MOBA_BLOCK, 0.0), dtype=BF16)
    step_w = HEADS_PER_STEP * HEAD_DIM
    pair_fm = pl.BlockSpec((1, step_w, s), lambda b, p: (b, p, 0))
    return pl.pallas_call(
        _attn_kernel,
        grid=(bsz, ATTN_HEADS // HEADS_PER_STEP),
        in_specs=[
            pair_fm,
            pl.BlockSpec((1, s, step_w), lambda b, p: (b, 0, p)),
            pair_fm,
            pl.BlockSpec((HEADS_PER_GROUP, s, LANES), lambda b, p: (0, 0, 0)),
            pl.BlockSpec((BF16_ROWS, s), lambda b, p: (0, 0)),
        ],
        out_specs=pair_fm,
        out_shape=jax.ShapeDtypeStruct((bsz, w, s), BF16),
        scratch_shapes=[pltpu.VMEM((HEADS_PER_STEP, s, LANES), BF16),
                        pltpu.VMEM((HEADS_PER_STEP, LANES, s), BF16),
                        pltpu.VMEM((HEADS_PER_STEP, HEAD_DIM + BF16_ROWS, s), BF16),
                        pltpu.VMEM((ATTN_RING, MOBA_BLOCK, MOBA_BLOCK), F32)],
        compiler_params=pltpu.CompilerParams(
            dimension_semantics=("arbitrary", "arbitrary"), vmem_limit_bytes=VMEM_LIMIT),
        name="moba_attention",
    )(qt, k, vt, oh, avg)


def _outproj_kernel(at_ref, zat_ref, yl_ref, x_ref, mod_ref, ga_ref, wo_ref, fg_ref, o_ref,
                    *, final_norm):
    d = x_ref.shape[2]
    ts = x_ref.shape[1]
    w = ATTN_WIDTH
    gate = mod_ref[0, :, 2 * d:3 * d]
    tn = (((0,), (0,)), ((), ()))
    subs = [slice(r, r + OUT_SUB) for r in range(0, ts, OUT_SUB)]
    ya, y = {}, {}

    def prep(i):
        at = at_ref[0, :, subs[i]].astype(F32)
        rs = lax.rsqrt(jnp.mean(at * at, axis=0, keepdims=True) + EPS)
        z = zat_ref[0, :, subs[i]].astype(F32)
        ya[i] = ((at * rs) * ga_ref[...] * (z * _sigmoid(z))).astype(BF16)

    def project(i):
        yy = lax.dot_general(ya[i], wo_ref[0:w, :], tn, preferred_element_type=F32)
        y[i] = yy + jnp.dot(yl_ref[0, subs[i], :], wo_ref[w:2 * w, :],
                            preferred_element_type=F32)

    def finish(i):
        xo = x_ref[0, subs[i], :] + gate * y[i]
        if final_norm:
            xo = (xo * lax.rsqrt(jnp.mean(xo * xo, axis=-1, keepdims=True) + EPS)) * fg_ref[...]
        o_ref[0, subs[i], :] = xo

    n = len(subs)
    prep(0)
    for i in range(n):
        if i + 1 < n:
            prep(i + 1)
        project(i)
        if i > 0:
            finish(i - 1)
    finish(n - 1)


def _outproj(at, zat, yl, x, mod3, gain_a, w_out, final_gain, ts, final_norm):
    bsz, s, d = x.shape
    w = ATTN_WIDTH
    return pl.pallas_call(
        functools.partial(_outproj_kernel, final_norm=final_norm),
        grid=(bsz, s // ts),
        in_specs=[
            pl.BlockSpec((1, w, ts), lambda b, i: (b, 0, i)),
            pl.BlockSpec((1, w, ts), lambda b, i: (b, 0, i)),
            pl.BlockSpec((1, ts, w), lambda b, i: (b, i, 0)),
            pl.BlockSpec((1, ts, d), lambda b, i: (b, i, 0)),
            pl.BlockSpec((1, 1, 3 * d), lambda b, i: (b, 0, 0)),
            pl.BlockSpec((w, 1), lambda b, i: (0, 0)),
            pl.BlockSpec((2 * w, d), lambda b, i: (0, 0)),
            pl.BlockSpec((1, d), lambda b, i: (0, 0)),
        ],
        out_specs=pl.BlockSpec((1, ts, d), lambda b, i: (b, i, 0)),
        out_shape=jax.ShapeDtypeStruct((bsz, s, d), F32),
        compiler_params=pltpu.CompilerParams(
            dimension_semantics=("arbitrary", "arbitrary"), vmem_limit_bytes=VMEM_LIMIT),
        name="outproj",
    )(at, zat, yl, x, mod3, gain_a, w_out, final_gain)


def _rope_tables(s):
    pos = np.arange(s, dtype=np.float64)
    inv_freq = ROPE_THETA ** (-np.arange(0, HEAD_DIM, 2, dtype=np.float64) / HEAD_DIM)
    ang = pos[:, None] * inv_freq[None, :]
    cos, sin = np.cos(ang), np.sin(ang)
    cosn = np.tile(cos, (1, LANES // HALF_DIM))
    sinn = np.tile(np.concatenate([-sin, sin], axis=1), (1, LANES // HEAD_DIM))
    cost = np.concatenate([cos.T, cos.T], axis=0)
    sint = np.concatenate([-sin.T, sin.T], axis=0)
    return tuple(jnp.asarray(t, dtype=F32) for t in (cosn, sinn, cost, sint))


def _block_diag(wg):
    g, n, _ = wg.shape
    eye = jnp.eye(g, dtype=wg.dtype)
    return jnp.einsum("gij,gh->gihj", wg, eye).reshape(g * n, g * n)


def kernel(x, c, w_mod, b_mod, norm_gain, w_in, conv_w, conv_b, w_rgate, b_rgate,
           w_igate, b_igate, lru_lambda, attn_out_gain, lru_out_gain, w_out, final_gain):
    bsz, s, d = x.shape
    depth = w_in.shape[0]
    w = ATTN_WIDTH
    assert s % OUT_TILE == 0 and d % LANES == 0 and w_in.shape[2] == 6 * w
    cosn, sinn, cost, sint = _rope_tables(s)
    for l in range(depth):
        mod3 = _modulation(c, w_mod[l], b_mod[l]).reshape(bsz, 1, 3 * d)
        wb, w_t = _weight_prep(w_in[l])
        qt, vt, zat, k, yl = _inproj_lru(
            x, mod3, norm_gain[l].reshape(1, d), wb, w_t, cosn, sinn, cost, sint,
            conv_w[l], conv_b[l],
            _block_diag(w_rgate[l].astype(BF16)), _block_diag(w_igate[l].astype(BF16)),
            b_rgate[l].reshape(-1), b_igate[l].reshape(-1), lru_lambda[l], lru_out_gain[l],
            IN_TILE)
        at = _attention(qt, k, vt)
        x = _outproj(at, zat, yl, x, mod3, attn_out_gain[l].reshape(w, 1),
                     w_out[l].astype(BF16), final_gain.reshape(1, d), OUT_TILE,
                     final_norm=(l == depth - 1))
    return x
```

```python
import functools

import jax
import jax.numpy as jnp
import numpy as np
from jax import lax
from jax.experimental import pallas as pl
from jax.experimental.pallas import tpu as pltpu

ATTN_HEADS = 8
HEAD_DIM = 64
HALF_DIM = HEAD_DIM // 2
ATTN_WIDTH = ATTN_HEADS * HEAD_DIM
LRU_BLOCKS = 8
MOBA_BLOCK = 256
MOBA_TOPK = 3
CONV_WIDTH = 4
LRU_C = 8.0
ROPE_THETA = 10000.0
EPS = 1e-6

LANES = 128
SUBLANES = 8
BF16_ROWS = 16
MASK_VALUE = -1e30
LOG2_E = 1.4426950408889634
VMEM_LIMIT = 56 * 1024 * 1024

F32 = jnp.float32
BF16 = jnp.bfloat16


def _sigmoid(z):
    return 0.5 * jnp.tanh(0.5 * z) + 0.5


def _split3(a):
    hi = a.astype(BF16)
    r1 = a - hi.astype(F32)
    mid = r1.astype(BF16)
    lo = (r1 - mid.astype(F32)).astype(BF16)
    return hi, mid, lo


def _mod_kernel(c_ref, w_ref, b_ref, o_ref):
    c = c_ref[...]
    act = c * _sigmoid(c)
    a_hi, a_mid, _ = _split3(act)
    w_hi, w_mid, _ = _split3(w_ref[...])
    dot = functools.partial(jnp.dot, preferred_element_type=F32)
    acc = dot(a_hi, w_hi) + dot(a_hi, w_mid) + dot(a_mid, w_hi)
    o_ref[...] = acc + b_ref[...]


def _modulation(c, w_mod, b_mod):
    bsz, d = c.shape
    n = w_mod.shape[1]
    tn = 1024
    return pl.pallas_call(
        _mod_kernel,
        grid=(n // tn,),
        in_specs=[
            pl.BlockSpec((bsz, d), lambda j: (0, 0)),
            pl.BlockSpec((d, tn), lambda j: (0, j)),
            pl.BlockSpec((1, tn), lambda j: (0, j)),
        ],
        out_specs=pl.BlockSpec((bsz, tn), lambda j: (0, j)),
        out_shape=jax.ShapeDtypeStruct((bsz, n), F32),
        compiler_params=pltpu.CompilerParams(
            dimension_semantics=("arbitrary",), vmem_limit_bytes=VMEM_LIMIT),
        name="modulation",
    )(c, w_mod, b_mod.reshape(1, n))


def _wprep_kernel(w_ref, wb_ref, wt_ref):
    j = pl.program_id(0)
    wblk = w_ref[...]
    wb_ref[...] = wblk.astype(BF16)

    @pl.when((j == Q_GROUP) | (j == V_GROUP) | (j == ZA_GROUP))
    def _():
        wt_ref[...] = wblk.T.astype(BF16)


def _weight_prep(w_in):
    d, n = w_in.shape
    w = ATTN_WIDTH
    t_block = lambda j: jnp.clip(j - 1, 0, 2)
    return pl.pallas_call(
        _wprep_kernel,
        grid=(n // w,),
        in_specs=[pl.BlockSpec((d, w), lambda j: (0, j))],
        out_specs=[pl.BlockSpec((d, w), lambda j: (0, j)),
                   pl.BlockSpec((w, d), lambda j: (t_block(j), 0))],
        out_shape=[jax.ShapeDtypeStruct((d, n), BF16), jax.ShapeDtypeStruct((3 * w, d), BF16)],
        compiler_params=pltpu.CompilerParams(
            dimension_semantics=("arbitrary",), vmem_limit_bytes=VMEM_LIMIT),
        name="weight_prep",
    )(w_in)


Q_GROUP, K_GROUP, V_GROUP, ZA_GROUP, XL_GROUP, ZL_GROUP = range(6)
IN_TILE = 1024
OUT_TILE = 2048
OUT_SUB = 256
K_COL = 1 * ATTN_WIDTH
XL_COL = 4 * ATTN_WIDTH
ZL_COL = 5 * ATTN_WIDTH
LRU_CHUNK = 128
PROJ_PIECE = 256
LRU_QUAD = 256
LP_CONV_B, LP_B_R, LP_B_I, LP_LAMBDA, LP_GAIN = range(CONV_WIDTH, CONV_WIDTH + 5)
LP_ROWS = 16


def _inproj_lru_kernel(x_ref, mod_ref, gain_ref, wn_ref, wt_ref, cosn_ref, sinn_ref,
                       cost_ref, sint_ref, lp_ref, wr_ref, wi_ref,
                       qt_ref, vt_ref, zat_ref, k_ref, yl_ref,
                       ext_buf, zl_buf, a_buf, u_buf, hc_ref):
    d = x_ref.shape[2]
    ts = x_ref.shape[1]
    w = ATTN_WIDTH
    pad = SUBLANES
    z0 = jnp.minimum(pl.program_id(1), 0)
    ext_ref, zl_ref, a_ref, u_ref = ext_buf.at[z0], zl_buf.at[z0], a_buf.at[z0], u_buf.at[z0]

    @pl.when(pl.program_id(1) == 0)
    def _():
        ext_ref[0:pad, :] = jnp.zeros((pad, w), F32)
        hc_ref[...] = jnp.zeros(hc_ref.shape, F32)

    x = x_ref[0]
    shift = mod_ref[0, :, 0:d]
    scale = mod_ref[0, :, d:2 * d]
    rs = lax.rsqrt(jnp.mean(x * x, axis=-1, keepdims=True) + EPS)
    h = (x * rs) * (gain_ref[...] * (1.0 + scale)) + shift
    hb = h.astype(BF16)

    ext_ref[pad:pad + ts, :] = jnp.dot(hb, wn_ref[:, XL_COL:XL_COL + w],
                                       preferred_element_type=F32)
    zl_ref[...] = jnp.dot(hb, wn_ref[:, ZL_COL:ZL_COL + w], preferred_element_type=F32)

    nt = (((1,), (1,)), ((), ()))
    half = PROJ_PIECE
    n_piece = w // PROJ_PIECE

    def k_piece(p):
        kk = jnp.dot(hb, wn_ref[:, K_COL + p * half:K_COL + (p + 1) * half],
                     preferred_element_type=F32)
        cosn = cosn_ref[...]
        sinn = sinn_ref[...]
        lane = lax.broadcasted_iota(jnp.int32, cosn.shape, 1)
        first_half = (lane % HEAD_DIM) < HALF_DIM
        for g in range(half // LANES):
            kg = kk[:, g * LANES:(g + 1) * LANES]
            swapped = jnp.where(first_half,
                                pltpu.roll(kg, LANES - HALF_DIM, axis=1),
                                pltpu.roll(kg, HALF_DIM, axis=1))
            lo = p * half + g * LANES
            k_ref[0, :, lo:lo + LANES] = (kg * cosn + swapped * sinn).astype(BF16)

    def q_piece(p):
        qt = lax.dot_general(wt_ref[p * half:(p + 1) * half, :], hb, nt,
                             preferred_element_type=F32)
        cost = cost_ref[...]
        sint = sint_ref[...]
        qscale = HEAD_DIM ** -0.5 * LOG2_E
        for hd in range(half // HEAD_DIM):
            t = qt[hd * HEAD_DIM:(hd + 1) * HEAD_DIM]
            swapped = jnp.concatenate([t[HALF_DIM:], t[:HALF_DIM]], axis=0)
            lo = p * half + hd * HEAD_DIM
            qt_ref[0, lo:lo + HEAD_DIM, :] = ((t * cost + swapped * sint) * qscale).astype(BF16)

    def fm_piece(out_ref, base, p):
        lo = p * half
        out_ref[0, lo:lo + half, :] = lax.dot_general(
            wt_ref[base + lo:base + lo + half, :], hb, nt,
            preferred_element_type=F32).astype(BF16)

    lam = lp_ref[LP_LAMBDA:LP_LAMBDA + 1, :]
    neg_lam = -lam
    softplus = jnp.maximum(neg_lam, 0.0) + jnp.log1p(jnp.exp(-jnp.abs(neg_lam)))
    decay = (-LRU_C * LOG2_E) * softplus
    state = {"h": hc_ref[...]}

    def quad_dot(xb, w_ref):
        return jnp.concatenate(
            [jnp.dot(xb[:, q:q + LRU_QUAD], w_ref[q:q + LRU_QUAD, q:q + LRU_QUAD],
                     preferred_element_type=F32) for q in range(0, w, LRU_QUAD)], axis=1)

    def lru_gates(c):
        base = pad + c * LRU_CHUNK
        xc = (lp_ref[LP_CONV_B:LP_CONV_B + 1, :] +
              lp_ref[CONV_WIDTH - 1:CONV_WIDTH, :] * ext_ref[base:base + LRU_CHUNK, :])
        for tap in range(1, CONV_WIDTH):
            xc = xc + (lp_ref[CONV_WIDTH - 1 - tap:CONV_WIDTH - tap, :] *
                       ext_ref[base - tap:base - tap + LRU_CHUNK, :])
        xcb = xc.astype(BF16)
        r = _sigmoid(quad_dot(xcb, wr_ref) + lp_ref[LP_B_R:LP_B_R + 1, :])
        ig = _sigmoid(quad_dot(xcb, wi_ref) + lp_ref[LP_B_I:LP_B_I + 1, :])
        a = jnp.exp2(decay * r)
        y = 1.0 - a * a
        u = jnp.where(y > 0.0, y * lax.rsqrt(y), 0.0) * ig * xc
        grp = (LRU_CHUNK // SUBLANES, SUBLANES, w)
        a = a.reshape(grp)
        u = u.reshape(grp)
        rowg = lax.broadcasted_iota(jnp.int32, grp, 1)
        for sh in (1, 2, 4):
            ok = rowg >= sh
            a_prev = pltpu.roll(a, sh, axis=1)
            u_prev = pltpu.roll(u, sh, axis=1)
            u = jnp.where(ok, a * u_prev + u, u)
            a = jnp.where(ok, a * a_prev, a)
        a_ref[c * LRU_CHUNK:(c + 1) * LRU_CHUNK, :] = a.reshape(LRU_CHUNK, w)
        u_ref[c * LRU_CHUNK:(c + 1) * LRU_CHUNK, :] = u.reshape(LRU_CHUNK, w)

    def lru_carry(c):
        hprev = state["h"]
        for g in range(LRU_CHUNK // SUBLANES):
            r0 = c * LRU_CHUNK + g * SUBLANES
            hcur = u_ref[r0:r0 + SUBLANES, :] + a_ref[r0:r0 + SUBLANES, :] * hprev
            u_ref[r0:r0 + SUBLANES, :] = hcur
            hprev = hcur[SUBLANES - 1:SUBLANES, :]
        state["h"] = hprev

    def lru_finish(c):
        rows = slice(c * LRU_CHUNK, (c + 1) * LRU_CHUNK)
        rec = u_ref[rows, :]
        rsq = lax.rsqrt(jnp.mean(rec * rec, axis=-1, keepdims=True) + EPS)
        z = zl_ref[rows, :]
        yl_ref[0, rows, :] = ((rec * rsq) * lp_ref[LP_GAIN:LP_GAIN + 1, :] *
                              (z * _sigmoid(z))).astype(BF16)

    mxu_pieces = ([functools.partial(k_piece, p) for p in range(n_piece)] +
                  [functools.partial(q_piece, p) for p in range(n_piece)] +
                  [functools.partial(fm_piece, vt_ref, w, p) for p in range(n_piece)] +
                  [functools.partial(fm_piece, zat_ref, 2 * w, p) for p in range(n_piece)])
    lru_pieces = []
    for c in range(ts // LRU_CHUNK):
        lru_pieces += [functools.partial(lru_gates, c), functools.partial(lru_carry, c),
                       functools.partial(lru_finish, c)]
    done = 0
    for i, piece in enumerate(mxu_pieces):
        piece()
        upto = (i + 1) * len(lru_pieces) // len(mxu_pieces)
        for lp in lru_pieces[done:upto]:
            lp()
        done = upto

    hc_ref[...] = state["h"]
    ext_ref[0:pad, :] = ext_ref[ts:ts + pad, :]


def _inproj_lru(x, mod3, gain, w_nat, w_t, cosn, sinn, cost, sint,
                conv_w, conv_b, wr_bd, wi_bd, b_r, b_i, lam, lru_gain, ts):
    bsz, s, d = x.shape
    w = ATTN_WIDTH
    fm = jax.ShapeDtypeStruct((bsz, w, s), BF16)
    sm = jax.ShapeDtypeStruct((bsz, s, w), BF16)
    fm_spec = pl.BlockSpec((1, w, ts), lambda b, i: (b, 0, i))
    sm_spec = pl.BlockSpec((1, ts, w), lambda b, i: (b, i, 0))
    mat_spec = pl.BlockSpec((w, w), lambda b, i: (0, 0))
    lru_params = jnp.concatenate(
        [conv_w] + [v.reshape(1, w) for v in (conv_b, b_r, b_i, lam, lru_gain)] +
        [jnp.zeros((LP_ROWS - LP_GAIN - 1, w), F32)], axis=0)
    return pl.pallas_call(
        _inproj_lru_kernel,
        grid=(bsz, s // ts),
        in_specs=[
            pl.BlockSpec((1, ts, d), lambda b, i: (b, i, 0)),
            pl.BlockSpec((1, 1, 3 * d), lambda b, i: (b, 0, 0)),
            pl.BlockSpec((1, d), lambda b, i: (0, 0)),
            pl.BlockSpec((d, 6 * w), lambda b, i: (0, 0)),
            pl.BlockSpec((3 * w, d), lambda b, i: (0, 0)),
            pl.BlockSpec((ts, LANES), lambda b, i: (i, 0)),
            pl.BlockSpec((ts, LANES), lambda b, i: (i, 0)),
            pl.BlockSpec((HEAD_DIM, ts), lambda b, i: (0, i)),
            pl.BlockSpec((HEAD_DIM, ts), lambda b, i: (0, i)),
            pl.BlockSpec((LP_ROWS, w), lambda b, i: (0, 0)), mat_spec, mat_spec,
        ],
        out_specs=[fm_spec, fm_spec, fm_spec, sm_spec, sm_spec],
        out_shape=[fm, fm, fm, sm, sm],
        scratch_shapes=[pltpu.VMEM((1, ts + SUBLANES, w), F32),
                        pltpu.VMEM((1, ts, w), F32),
                        pltpu.VMEM((1, ts, w), F32),
                        pltpu.VMEM((1, ts, w), F32),
                        pltpu.VMEM((1, w), F32)],
        compiler_params=pltpu.CompilerParams(
            dimension_semantics=("arbitrary", "arbitrary"), vmem_limit_bytes=VMEM_LIMIT),
        name="inproj_rglru",
    )(x, mod3, gain, w_nat, w_t, cosn, sinn, cost, sint,
      lru_params, wr_bd, wi_bd)


HEADS_PER_GROUP = LANES // HEAD_DIM
HEADS_PER_STEP = 4
BLOCK_SHIFT = MOBA_BLOCK.bit_length() - 1
assert 1 << BLOCK_SHIFT == MOBA_BLOCK
ATTN_LEAD = 5
ATTN_RING = 16


def _attn_kernel(qt_ref, k_ref, vt_ref, oh_ref, avg_ref, o_ref, ka_ref, qa_ref, va_ref, s_ref):
    s = qt_ref.shape[2]
    nblk = s // MOBA_BLOCK

    jrow = lax.broadcasted_iota(jnp.int32, (nblk, s), 0)
    qblk = lax.shift_right_logical(lax.broadcasted_iota(jnp.int32, (nblk, s), 1), BLOCK_SHIFT)
    cand = jrow < qblk
    qb = qblk[0:1, :]
    kr = lax.broadcasted_iota(jnp.int32, (MOBA_BLOCK, MOBA_BLOCK), 0)
    qc = lax.broadcasted_iota(jnp.int32, (MOBA_BLOCK, MOBA_BLOCK), 1)
    causal = kr <= qc

    for hh in range(HEADS_PER_STEP):
        grp, hp = divmod(hh, HEADS_PER_GROUP)
        grp_lanes = slice(grp * LANES, (grp + 1) * LANES)
        head_lo = hp * HEAD_DIM
        aux_lo = HEAD_DIM - head_lo
        rows_l = slice(head_lo, head_lo + HEAD_DIM)
        rows_h = slice(hh * HEAD_DIM, (hh + 1) * HEAD_DIM)

        if hp == 0:
            kmean = jnp.dot(avg_ref[...], k_ref[0, :, grp_lanes], preferred_element_type=F32)
            km3 = jnp.concatenate(_split3(kmean), axis=0)

        ka_ref[hh] = oh_ref[hp]
        ka_ref[hh, :, rows_l] = k_ref[0, :, grp * LANES + head_lo:grp * LANES + head_lo + HEAD_DIM]

        va_ref[hh, 0:HEAD_DIM, :] = vt_ref[0, rows_h, :]
        va_ref[hh, HEAD_DIM:, :] = jnp.ones((BF16_ROWS, s), BF16)

        qa_ref[hh] = jnp.zeros(qa_ref.shape[1:], BF16)
        qa_ref[hh, rows_l, :] = qt_ref[0, rows_h, :]
        g3 = jnp.dot(km3, qa_ref[hh], preferred_element_type=F32)
        gate = (g3[0:BF16_ROWS] + g3[BF16_ROWS:2 * BF16_ROWS]) + g3[2 * BF16_ROWS:]
        gate = gate[0:nblk]

        bias_rows = []
        for j in range(nblk):
            gj = gate[j:j + 1, :]
            beats = cand & ((gate > gj) | ((gate == gj) & (jrow < j)))
            rank = jnp.sum(beats.astype(F32), axis=0, keepdims=True)
            keep = ((qb > j) & (rank < MOBA_TOPK)) | (qb == j)
            bias_rows.append(jnp.where(keep, 0.0, MASK_VALUE))
        bias = jnp.concatenate(bias_rows + [jnp.zeros((BF16_ROWS - nblk, s), F32)], axis=0)
        qa_ref[hh, aux_lo:aux_lo + BF16_ROWS, :] = bias.astype(BF16)

    items = [(i, hh) for hh in range(HEADS_PER_STEP)
             for i in (range(nblk) if hh % 2 == 0 else reversed(range(nblk)))]
    piece_list = [(n, j) for n, (i, _) in enumerate(items) for j in range(i + 1)]
    item_end = {}
    for t, (n, _) in enumerate(piece_list):
        item_end[n] = t + 1
    state = [dict() for _ in items]
    ring = s_ref.shape[0]
    slot0 = jnp.minimum(pl.program_id(1), 0)

    def blk_rows(j):
        return slice(j * MOBA_BLOCK, (j + 1) * MOBA_BLOCK)

    def score_piece(t):
        n, j = piece_list[t]
        i, hh = items[n]
        sc = jnp.dot(ka_ref[hh, blk_rows(j), :], qa_ref[hh, :, blk_rows(i)],
                     preferred_element_type=F32)
        if j == i:
            sc = jnp.where(causal, sc, MASK_VALUE)
        s_ref[slot0 + t % ring] = sc
        cm = jnp.max(sc, axis=0, keepdims=True)
        st = state[n]
        st["m"] = cm if j == 0 else jnp.maximum(st["m"], cm)

    def softmax_piece(t):
        n, j = piece_list[t]
        i, hh = items[n]
        rows_h = slice(hh * HEAD_DIM, (hh + 1) * HEAD_DIM)
        st = state[n]
        p = jnp.exp2(s_ref[slot0 + t % ring] - st["m"])
        pv = jnp.dot(va_ref[hh, :, blk_rows(j)], p.astype(BF16),
                     preferred_element_type=F32)
        st["acc"] = pv if j == 0 else st["acc"] + pv
        if j == i:
            acc = st["acc"]
            o_ref[0, rows_h, blk_rows(i)] = (
                acc[0:HEAD_DIM] / acc[HEAD_DIM:HEAD_DIM + 1]).astype(BF16)

    total = len(piece_list)
    scored = softmaxed = 0
    while softmaxed < total:
        if scored < total:
            assert scored - softmaxed < ring, "score ring too small for the softmax lag"
            score_piece(scored)
            scored += 1
        ready = min(total, item_end[piece_list[softmaxed][0]] + ATTN_LEAD)
        if scored >= ready:
            softmax_piece(softmaxed)
            softmaxed += 1


def _attention(qt, k, vt):
    bsz, w, s = qt.shape
    blk = np.arange(s) // MOBA_BLOCK
    lane = np.arange(LANES)
    oh = jnp.asarray(np.stack([(lane[None, :] == (HEAD_DIM - hp * HEAD_DIM) + blk[:, None])
                               for hp in range(HEADS_PER_GROUP)]), dtype=BF16)
    avg = jnp.asarray(np.where(np.arange(BF16_ROWS)[:, None] == blk[None, :],
                               1.0 / MOBA_BLOCK, 0.0), dtype=BF16)
    step_w = HEADS_PER_STEP * HEAD_DIM
    pair_fm = pl.BlockSpec((1, step_w, s), lambda b, p: (b, p, 0))
    return pl.pallas_call(
        _attn_kernel,
        grid=(bsz, ATTN_HEADS // HEADS_PER_STEP),
        in_specs=[
            pair_fm,
            pl.BlockSpec((1, s, step_w), lambda b, p: (b, 0, p)),
            pair_fm,
            pl.BlockSpec((HEADS_PER_GROUP, s, LANES), lambda b, p: (0, 0, 0)),
            pl.BlockSpec((BF16_ROWS, s), lambda b, p: (0, 0)),
        ],
        out_specs=pair_fm,
        out_shape=jax.ShapeDtypeStruct((bsz, w, s), BF16),
        scratch_shapes=[pltpu.VMEM((HEADS_PER_STEP, s, LANES), BF16),
                        pltpu.VMEM((HEADS_PER_STEP, LANES, s), BF16),
                        pltpu.VMEM((HEADS_PER_STEP, HEAD_DIM + BF16_ROWS, s), BF16),
                        pltpu.VMEM((ATTN_RING, MOBA_BLOCK, MOBA_BLOCK), F32)],
        compiler_params=pltpu.CompilerParams(
            dimension_semantics=("arbitrary", "arbitrary"), vmem_limit_bytes=VMEM_LIMIT),
        name="moba_attention",
    )(qt, k, vt, oh, avg)


def _outproj_kernel(at_ref, zat_ref, yl_ref, x_ref, mod_ref, ga_ref, wo_ref, fg_ref, o_ref,
                    *, final_norm):
    d = x_ref.shape[2]
    ts = x_ref.shape[1]
    w = ATTN_WIDTH
    gate = mod_ref[0, :, 2 * d:3 * d]
    tn = (((0,), (0,)), ((), ()))
    subs = [slice(r, r + OUT_SUB) for r in range(0, ts, OUT_SUB)]
    ya, y = {}, {}

    def prep(i):
        at = at_ref[0, :, subs[i]].astype(F32)
        rs = lax.rsqrt(jnp.mean(at * at, axis=0, keepdims=True) + EPS)
        z = zat_ref[0, :, subs[i]].astype(F32)
        ya[i] = ((at * rs) * ga_ref[...] * (z * _sigmoid(z))).astype(BF16)

    def project(i):
        yy = lax.dot_general(ya[i], wo_ref[0:w, :], tn, preferred_element_type=F32)
        y[i] = yy + jnp.dot(yl_ref[0, subs[i], :], wo_ref[w:2 * w, :],
                            preferred_element_type=F32)

    def finish(i):
        xo = x_ref[0, subs[i], :] + gate * y[i]
        if final_norm:
            xo = (xo * lax.rsqrt(jnp.mean(xo * xo, axis=-1, keepdims=True) + EPS)) * fg_ref[...]
        o_ref[0, subs[i], :] = xo

    n = len(subs)
    prep(0)
    for i in range(n):
        if i + 1 < n:
            prep(i + 1)
        project(i)
        if i > 0:
            finish(i - 1)
    finish(n - 1)


def _outproj(at, zat, yl, x, mod3, gain_a, w_out, final_gain, ts, final_norm):
    bsz, s, d = x.shape
    w = ATTN_WIDTH
    return pl.pallas_call(
        functools.partial(_outproj_kernel, final_norm=final_norm),
        grid=(bsz, s // ts),
        in_specs=[
            pl.BlockSpec((1, w, ts), lambda b, i: (b, 0, i)),
            pl.BlockSpec((1, w, ts), lambda b, i: (b, 0, i)),
            pl.BlockSpec((1, ts, w), lambda b, i: (b, i, 0)),
            pl.BlockSpec((1, ts, d), lambda b, i: (b, i, 0)),
            pl.BlockSpec((1, 1, 3 * d), lambda b, i: (b, 0, 0)),
            pl.BlockSpec((w, 1), lambda b, i: (0, 0)),
            pl.BlockSpec((2 * w, d), lambda b, i: (0, 0)),
            pl.BlockSpec((1, d), lambda b, i: (0, 0)),
        ],
        out_specs=pl.BlockSpec((1, ts, d), lambda b, i: (b, i, 0)),
        out_shape=jax.ShapeDtypeStruct((bsz, s, d), F32),
        compiler_params=pltpu.CompilerParams(
            dimension_semantics=("arbitrary", "arbitrary"), vmem_limit_bytes=VMEM_LIMIT),
        name="outproj",
    )(at, zat, yl, x, mod3, gain_a, w_out, final_gain)


def _rope_tables(s):
    pos = np.arange(s, dtype=np.float64)
    inv_freq = ROPE_THETA ** (-np.arange(0, HEAD_DIM, 2, dtype=np.float64) / HEAD_DIM)
    ang = pos[:, None] * inv_freq[None, :]
    cos, sin = np.cos(ang), np.sin(ang)
    cosn = np.tile(cos, (1, LANES // HALF_DIM))
    sinn = np.tile(np.concatenate([-sin, sin], axis=1), (1, LANES // HEAD_DIM))
    cost = np.concatenate([cos.T, cos.T], axis=0)
    sint = np.concatenate([-sin.T, sin.T], axis=0)
    return tuple(jnp.asarray(t, dtype=F32) for t in (cosn, sinn, cost, sint))


def _block_diag(wg):
    g, n, _ = wg.shape
    eye = jnp.eye(g, dtype=wg.dtype)
    return jnp.einsum("gij,gh->gihj", wg, eye).reshape(g * n, g * n)


def kernel(x, c, w_mod, b_mod, norm_gain, w_in, conv_w, conv_b, w_rgate, b_rgate,
           w_igate, b_igate, lru_lambda, attn_out_gain, lru_out_gain, w_out, final_gain):
    bsz, s, d = x.shape
    depth = w_in.shape[0]
    w = ATTN_WIDTH
    assert s % OUT_TILE == 0 and d % LANES == 0 and w_in.shape[2] == 6 * w
    cosn, sinn, cost, sint = _rope_tables(s)
    for l in range(depth):
        mod3 = _modulation(c, w_mod[l], b_mod[l]).reshape(bsz, 1, 3 * d)
        wb, w_t = _weight_prep(w_in[l])
        qt, vt, zat, k, yl = _inproj_lru(
            x, mod3, norm_gain[l].reshape(1, d), wb, w_t, cosn, sinn, cost, sint,
            conv_w[l], conv_b[l],
            _block_diag(w_rgate[l].astype(BF16)), _block_diag(w_igate[l].astype(BF16)),
            b_rgate[l].reshape(-1), b_igate[l].reshape(-1), lru_lambda[l], lru_out_gain[l],
            IN_TILE)
        at = _attention(qt, k, vt)
        x = _outproj(at, zat, yl, x, mod3, attn_out_gain[l].reshape(w, 1),
                     w_out[l].astype(BF16), final_gain.reshape(1, d), OUT_TILE,
                     final_norm=(l == depth - 1))
    return x
```
